```python
import math
import jax, jax.numpy as jnp
from jax import lax
import numpy as np

D_MODEL = 1024
BATCH = 8
SEQ = 2048
DEPTH = 1

N_MOD = 6
NORM_EPS = 1e-6

A_GROUPS = ((128, 1), (512, 4), (2048, 16))
A_N_GROUPS = 3
A_HEADS_PER_GROUP = 4
A_HEAD_DIM = 128
A_WIDTH = A_N_GROUPS * A_HEADS_PER_GROUP * A_HEAD_DIM
A_MERGED_WIDTH = A_HEADS_PER_GROUP * A_HEAD_DIM

B_HEADS = 8
B_HEAD_K = 128
B_HEAD_V = 128
B_KEY_WIDTH = B_HEADS * B_HEAD_K
B_VAL_WIDTH = B_HEADS * B_HEAD_V
B_CONV = 4
B_CHUNK = 64

MOE_GROUPS = 4
MOE_EXPERTS_PER_GROUP = 4
MOE_N_EXPERTS = MOE_GROUPS * MOE_EXPERTS_PER_GROUP
MOE_TOP_K = 2
MOE_D_FF = 256

IN_SPLITS = (A_WIDTH, A_WIDTH, A_WIDTH, B_KEY_WIDTH, B_KEY_WIDTH, B_VAL_WIDTH, B_VAL_WIDTH, B_HEADS, B_HEADS, D_MODEL, D_MODEL)
D_IN = sum(IN_SPLITS)

kernel_name = 'hybrid_dilated_gdn_hmoe_block'


def rms_norm(x, w):
    xf = x.astype(jnp.float32)
    y = xf * lax.rsqrt(jnp.mean(xf * xf, axis=-1, keepdims=True) + NORM_EPS)
    return (y * w.astype(jnp.float32)).astype(x.dtype)


def l2_norm(x):
    xf = x.astype(jnp.float32)
    return xf * lax.rsqrt(jnp.sum(xf * xf, axis=-1, keepdims=True) + NORM_EPS)


def _dilated_window_group(q, k, v, window, dilation):
    B, S, H, dh = q.shape
    span = window // dilation
    L = S // dilation
    nb = -(-L // span)
    Lp = nb * span

    def to_sub(t):
        e = t.shape[-1]
        t = t.reshape(B, L, dilation, H, e).transpose(0, 2, 3, 1, 4)
        t = jnp.pad(t, ((0, 0), (0, 0), (0, 0), (0, Lp - L), (0, 0)))
        return t.reshape(B, dilation, H, nb, span, e)

    def with_prev(t):
        prev = jnp.pad(t[:, :, :, :-1], ((0, 0), (0, 0), (0, 0), (1, 0), (0, 0), (0, 0)))
        return jnp.concatenate([prev, t], axis=4)

    qb = to_sub(q).astype(jnp.float32)
    kw = with_prev(to_sub(k)).astype(jnp.float32)
    vw = with_prev(to_sub(v))
    s = jnp.einsum('bdhnqe,bdhnke->bdhnqk', qb, kw) * (dh ** -0.5)
    qi = jnp.arange(span)[:, None]
    kj = jnp.arange(2 * span)[None, :]
    dist = qi + span - kj
    band = (dist >= 0) & (dist <= span)
    exists = (jnp.arange(nb)[:, None, None] > 0) | (kj[None] >= span)
    mask = band[None] & exists
    s = jnp.where(mask, s, -jnp.inf)
    m = jnp.max(s, axis=-1, keepdims=True)
    p = jnp.exp(s - m)
    den = jnp.sum(p, axis=-1, keepdims=True)
    o = jnp.einsum('bdhnqk,bdhnke->bdhnqe', p, vw) / den
    lse = (m + jnp.log(den))[..., 0]
    o = o.reshape(B, dilation, H, Lp, dh)[:, :, :, :L].transpose(0, 3, 1, 2, 4).reshape(B, S, H, dh)
    lse = lse.reshape(B, dilation, H, Lp)[..., :L].transpose(0, 3, 1, 2).reshape(B, S, H)
    return o, lse


def _dilated_attention_branch(qa, ka, va, q_norm_w, k_norm_w):
    B, S, _ = qa.shape
    shp = (B, S, A_N_GROUPS, A_HEADS_PER_GROUP, A_HEAD_DIM)
    q = rms_norm(qa.reshape(shp), q_norm_w)
    k = rms_norm(ka.reshape(shp), k_norm_w)
    v = va.reshape(shp)
    outs, lses = [], []
    for g, (window, dilation) in enumerate(A_GROUPS):
        o_g, lse_g = _dilated_window_group(q[:, :, g], k[:, :, g], v[:, :, g], window, dilation)
        outs.append(o_g)
        lses.append(lse_g)
    o = jnp.stack(outs, axis=0)
    lse = jnp.stack(lses, axis=0)
    w = jax.nn.softmax(lse, axis=0)
    out = jnp.sum(w[..., None] * o, axis=0)
    return out.reshape(B, S, A_MERGED_WIDTH).astype(qa.dtype)


def _causal_depthwise_conv(x, w):
    K, C = w.shape
    return lax.conv_general_dilated(x, w[:, None, :].astype(x.dtype), window_strides=(1,),
                                    padding=((K - 1, 0),), dimension_numbers=('NWC', 'WIO', 'NWC'),
                                    feature_group_count=C)


def _chunk_gated_delta_rule(q, k, v, g, beta):
    f32 = jnp.float32
    B, S, H, dk = q.shape
    dv = v.shape[-1]
    C = B_CHUNK
    N = S // C

    def chunks4(t):
        return t.astype(f32).reshape(B, N, C, H, t.shape[-1]).transpose(1, 0, 3, 2, 4)

    def chunks3(t):
        return t.astype(f32).reshape(B, N, C, H).transpose(1, 0, 3, 2)

    q = chunks4(q) * (dk ** -0.5)
    k = chunks4(k)
    v = chunks4(v)
    bt = chunks3(beta)
    gc = jnp.cumsum(chunks3(g), axis=-1)
    causal = jnp.tril(jnp.ones((C, C), dtype=bool))
    strict = jnp.tril(jnp.ones((C, C), dtype=bool), -1)
    diff = gc[..., :, None] - gc[..., None, :]
    decay = jnp.where(causal, jnp.exp(jnp.where(causal, diff, 0.0)), 0.0)
    kbeta = k * bt[..., None]
    lower = jnp.where(strict, jnp.einsum('nbhik,nbhjk->nbhij', kbeta, k) * decay, 0.0)
    rhs = jnp.concatenate([v * bt[..., None], kbeta * jnp.exp(gc)[..., None]], axis=-1)
    sol = lax.linalg.triangular_solve(lower + jnp.eye(C, dtype=f32), rhs, left_side=True,
                                      lower=True, unit_diagonal=True)
    u, w = sol[..., :dv], sol[..., dv:]
    intra = jnp.where(causal, jnp.einsum('nbhik,nbhjk->nbhij', q, k) * decay, 0.0)
    g_last = gc[..., -1]
    k_to_end = k * jnp.exp(g_last[..., None] - gc)[..., None]
    q_decay = q * jnp.exp(gc)[..., None]

    def step(state, xs):
        u_n, w_n, qd_n, intra_n, kend_n, glast_n = xs
        v_new = u_n - jnp.einsum('bhck,bhkv->bhcv', w_n, state)
        o_n = jnp.einsum('bhck,bhkv->bhcv', qd_n, state) + jnp.einsum('bhij,bhjv->bhiv', intra_n, v_new)
        state = state * jnp.exp(glast_n)[..., None, None] + jnp.einsum('bhck,bhcv->bhkv', kend_n, v_new)
        return state, o_n

    state0 = jnp.zeros((B, H, dk, dv), f32)
    _, o = lax.scan(step, state0, (u, w, q_decay, intra, k_to_end, g_last))
    return o.transpose(1, 0, 3, 2, 4).reshape(B, S, H, dv)


def _gated_deltanet_branch(qb, kb, vb, zb, beta_raw, a_raw, conv_w, A_log, dt_bias, out_norm_w):
    B, S, _ = qb.shape
    qkv = jax.nn.silu(_causal_depthwise_conv(jnp.concatenate([qb, kb, vb], axis=-1), conv_w))
    q, k, v = jnp.split(qkv, [B_KEY_WIDTH, 2 * B_KEY_WIDTH], axis=-1)
    q = l2_norm(q.reshape(B, S, B_HEADS, B_HEAD_K))
    k = l2_norm(k.reshape(B, S, B_HEADS, B_HEAD_K))
    v = v.reshape(B, S, B_HEADS, B_HEAD_V)
    beta = jax.nn.sigmoid(beta_raw.astype(jnp.float32))
    g = -jnp.exp(A_log.astype(jnp.float32)) * jax.nn.softplus(a_raw.astype(jnp.float32) + dt_bias.astype(jnp.float32))
    o = _chunk_gated_delta_rule(q, k, v, g, beta)
    o = rms_norm(o, out_norm_w) * jax.nn.silu(zb.reshape(B, S, B_HEADS, B_HEAD_V).astype(jnp.float32))
    return o.reshape(B, S, B_VAL_WIDTH).astype(qb.dtype)


def _token_mixers(h, w_in, conv_w, a_q_norm_w, a_k_norm_w, b_A_log, b_dt_bias, b_out_norm_w,
                  w_branch_a, w_branch_b, w_o):
    proj = jnp.einsum('bsd,de->bse', h, w_in)
    offsets = [int(o) for o in np.cumsum(IN_SPLITS)[:-1]]
    qa, ka, va, qb, kb, vb, zb, beta_raw, a_raw, gate_a_raw, gate_b_raw = jnp.split(proj, offsets, axis=-1)
    ya = _dilated_attention_branch(qa, ka, va, a_q_norm_w, a_k_norm_w)
    yb = _gated_deltanet_branch(qb, kb, vb, zb, beta_raw, a_raw, conv_w, b_A_log, b_dt_bias, b_out_norm_w)
    merged = (jax.nn.sigmoid(gate_a_raw) * jnp.einsum('bse,ed->bsd', ya, w_branch_a)
              + jax.nn.sigmoid(gate_b_raw) * jnp.einsum('bse,ed->bsd', yb, w_branch_b))
    return jnp.einsum('bsd,de->bse', merged, w_o)


def _hierarchical_moe(h, w_router_group, b_router_group, w_router_expert, b_router_expert,
                      w_gate, w_up, w_down):
    f32 = jnp.float32
    B, S, D = h.shape
    ht = h.reshape(B * S, D)
    T = ht.shape[0]
    group_logits = jnp.dot(ht, w_router_group).astype(f32) + b_router_group.astype(f32)
    p_group = jax.nn.softmax(group_logits, axis=-1)
    p_top_group, g_idx = lax.top_k(p_group, 1)
    expert_logits = (jnp.dot(ht, w_router_expert).astype(f32).reshape(T, MOE_GROUPS, MOE_EXPERTS_PER_GROUP)
                     + b_router_expert.astype(f32))
    in_group = jnp.einsum('tg,tge->te', jax.nn.one_hot(g_idx[:, 0], MOE_GROUPS, dtype=f32), expert_logits)
    top_logits, e_idx = lax.top_k(in_group, MOE_TOP_K)
    weights = p_top_group * jax.nn.softmax(top_logits, axis=-1)
    expert_ids = g_idx * MOE_EXPERTS_PER_GROUP + e_idx
    combine = jnp.sum(jax.nn.one_hot(expert_ids, MOE_N_EXPERTS, dtype=f32) * weights[..., None], axis=1)

    def expert(acc, xs):
        wg, wu, wd, cw = xs
        hidden = jax.nn.silu(ht @ wg) * (ht @ wu)
        return acc + (hidden @ wd).astype(f32) * cw[:, None], None

    acc, _ = lax.scan(expert, jnp.zeros((T, D), f32), (w_gate, w_up, w_down, combine.T))
    return acc.reshape(B, S, D).astype(h.dtype)


def setup_inputs(seed: int = 0) -> dict:
    key = jax.random.key(seed)
    ks = jax.random.split(key, 24)
    f32 = jnp.float32
    L = DEPTH
    D = D_MODEL

    def normal(k, shape, scale):
        return jax.random.normal(k, shape, f32) * scale

    def gain(k, shape):
        return 1.0 + 0.02 * jax.random.normal(k, shape, f32)

    dt = jnp.exp(jax.random.uniform(ks[9], (L, B_HEADS), f32, math.log(1e-3), math.log(1e-1)))
    return {
        'x': normal(ks[0], (BATCH, SEQ, D), 1.0),
        'c': normal(ks[1], (BATCH, D), 1.0),
        'w_ada': normal(ks[2], (L, D, N_MOD * D), 0.5 * D ** -0.5),
        'b_ada': normal(ks[3], (L, N_MOD * D), 0.02),
        'norm1_w': gain(ks[4], (L, D)),
        'w_in': normal(ks[5], (L, D, D_IN), D ** -0.5),
        'conv_w': normal(ks[6], (L, B_CONV, 2 * B_KEY_WIDTH + B_VAL_WIDTH), B_CONV ** -0.5),
        'a_q_norm_w': gain(ks[7], (L, A_HEAD_DIM)),
        'a_k_norm_w': gain(ks[8], (L, A_HEAD_DIM)),
        'b_A_log': jnp.log(jax.random.uniform(ks[10], (L, B_HEADS), f32, 1.0, 16.0)),
        'b_dt_bias': dt + jnp.log(-jnp.expm1(-dt)),
        'b_out_norm_w': gain(ks[11], (L, B_HEAD_V)),
        'w_branch_a': normal(ks[12], (L, A_MERGED_WIDTH, D), A_MERGED_WIDTH ** -0.5),
        'w_branch_b': normal(ks[13], (L, B_VAL_WIDTH, D), B_VAL_WIDTH ** -0.5),
        'w_o': normal(ks[14], (L, D, D), D ** -0.5),
        'norm2_w': gain(ks[15], (L, D)),
        'w_router_group': normal(ks[16], (L, D, MOE_GROUPS), D ** -0.5),
        'b_router_group': normal(ks[17], (L, MOE_GROUPS), 0.01),
        'w_router_expert': normal(ks[18], (L, D, MOE_N_EXPERTS), D ** -0.5),
        'b_router_expert': normal(ks[19], (L, MOE_GROUPS, MOE_EXPERTS_PER_GROUP), 0.01),
        'w_exp_gate': normal(ks[20], (L, MOE_N_EXPERTS, D, MOE_D_FF), D ** -0.5),
        'w_exp_up': normal(ks[21], (L, MOE_N_EXPERTS, D, MOE_D_FF), D ** -0.5),
        'w_exp_down': normal(ks[22], (L, MOE_N_EXPERTS, MOE_D_FF, D), MOE_D_FF ** -0.5),
    }


def reference(x, c, w_ada, b_ada, norm1_w, w_in, conv_w, a_q_norm_w, a_k_norm_w, b_A_log, b_dt_bias,
              b_out_norm_w, w_branch_a, w_branch_b, w_o, norm2_w, w_router_group, b_router_group,
              w_router_expert, b_router_expert, w_exp_gate, w_exp_up, w_exp_down):
    for i in range(DEPTH):
        mod = (jnp.dot(jax.nn.silu(c), w_ada[i]) + b_ada[i])[:, None, :]
        shift1, scale1, gate1, shift2, scale2, gate2 = jnp.split(mod, N_MOD, axis=-1)
        h = rms_norm(x, norm1_w[i]) * (1 + scale1) + shift1
        x = x + gate1 * _token_mixers(h, w_in[i], conv_w[i], a_q_norm_w[i], a_k_norm_w[i], b_A_log[i],
                                      b_dt_bias[i], b_out_norm_w[i], w_branch_a[i], w_branch_b[i], w_o[i])
        h2 = rms_norm(x, norm2_w[i]) * (1 + scale2) + shift2
        x = x + gate2 * _hierarchical_moe(h2, w_router_group[i], b_router_group[i], w_router_expert[i],
                                          b_router_expert[i], w_exp_gate[i], w_exp_up[i], w_exp_down[i])
    return x
```

```python
import functools

import jax
import jax.numpy as jnp
import numpy as np
from jax import lax
from jax.experimental import pallas as pl
from jax.experimental.pallas import tpu as pltpu

F32 = jnp.float32
BF16 = jnp.bfloat16

NORM_EPS = 1e-6
N_MOD = 6

A_GROUPS = ((128, 1), (512, 4), (2048, 16))
A_HEADS_PER_GROUP = 4
HEAD_DIM = 128
A_GROUP_WIDTH = A_HEADS_PER_GROUP * HEAD_DIM
A_GROUP_COLS = 3 * A_GROUP_WIDTH
A_SPAN = 128

B_HEADS = 8
B_CONV = 4
B_CHUNK = 64
B_HEAD_COLS = 4 * HEAD_DIM
GDN_HEADS_PER_STEP = 4
GDN_A_LANE = 8

MOE_GROUPS = 4
MOE_EXPERTS_PER_GROUP = 4
MOE_D_FF = 256
ROUTER_ROWS = 32
ROUTER_EXPERT_ROW0 = 8

GDN_COLS = B_HEADS * B_HEAD_COLS
GATE_COL0 = GDN_COLS
ATTN_COL0 = GDN_COLS + 2048
PROJ_COLS = ATTN_COL0 + 3 * A_GROUP_COLS

LANES = 128
VMEM_LIMIT_BYTES = 56 * 1024 * 1024

_NT = (((1,), (1,)), ((), ()))
_TN = (((0,), (0,)), ((), ()))


def _sigmoid(x):
    return 1.0 / (1.0 + jnp.exp(-x))


def _silu(x):
    return x * _sigmoid(x)


def _cparams(sem):
    return pltpu.CompilerParams(dimension_semantics=sem, vmem_limit_bytes=VMEM_LIMIT_BYTES)


def _ada_kernel(c_ref, w_ref, b_ref, o_ref):
    c = c_ref[...]
    o_ref[...] = jnp.dot(_silu(c), w_ref[...], preferred_element_type=F32,
                         precision=lax.Precision.HIGHEST) + b_ref[...]


def _ada(c, w_ada, b_ada):
    bsz, d = c.shape
    n = w_ada.shape[1]
    tn = 1536
    return pl.pallas_call(
        _ada_kernel,
        grid=(n // tn,),
        in_specs=[pl.BlockSpec((bsz, d), lambda j: (0, 0)),
                  pl.BlockSpec((d, tn), lambda j: (0, j)),
                  pl.BlockSpec((1, tn), lambda j: (0, j))],
        out_specs=pl.BlockSpec((bsz, tn), lambda j: (0, j)),
        out_shape=jax.ShapeDtypeStruct((bsz, n), F32),
        compiler_params=_cparams(("arbitrary",)),
        name="ada",
    )(c, w_ada, b_ada.reshape(1, n))


def _modulated_norm(x, norm_w, scale, shift):
    y = x * lax.rsqrt(jnp.mean(x * x, axis=-1, keepdims=True) + NORM_EPS)
    return (y * norm_w) * (1.0 + scale) + shift


def _inproj_kernel(x_ref, mod_ref, nw_ref, w_ref, wba_ref, proj_ref, ba_ref, h_ref):
    @pl.when(pl.program_id(2) == 0)
    def _():
        m = mod_ref[0]
        h = _modulated_norm(x_ref[0], nw_ref[...], m[1:2], m[0:1]).astype(BF16)
        h_ref[...] = h
        ba_ref[0] = jnp.dot(h, wba_ref[...], preferred_element_type=F32)

    proj_ref[0] = jnp.dot(h_ref[...], w_ref[...], preferred_element_type=F32).astype(BF16)


def _inproj(x, mod, norm_w, w1, wba, tm=1024, tn=1536):
    bsz, s, d = x.shape
    n = w1.shape[1]
    nba = wba.shape[1]
    return pl.pallas_call(
        _inproj_kernel,
        grid=(bsz, s // tm, n // tn),
        in_specs=[pl.BlockSpec((1, tm, d), lambda b, i, j: (b, i, 0)),
                  pl.BlockSpec((1, N_MOD, d), lambda b, i, j: (b, 0, 0)),
                  pl.BlockSpec((1, d), lambda b, i, j: (0, 0)),
                  pl.BlockSpec((d, tn), lambda b, i, j: (0, j)),
                  pl.BlockSpec((d, nba), lambda b, i, j: (0, 0))],
        out_specs=[pl.BlockSpec((1, tm, tn), lambda b, i, j: (b, i, j)),
                   pl.BlockSpec((1, tm, nba), lambda b, i, j: (b, i, 0))],
        out_shape=[jax.ShapeDtypeStruct((bsz, s, n), BF16),
                   jax.ShapeDtypeStruct((bsz, s, nba), F32)],
        scratch_shapes=[pltpu.VMEM((tm, d), BF16)],
        compiler_params=_cparams(("arbitrary", "arbitrary", "arbitrary")),
        name="inproj",
    )(x, mod, norm_w.reshape(1, d), w1, wba)


def _attn_kernel(*refs, dil, length):
    qkv_refs = refs[:dil]
    qw_ref, kw_ref, o_ref, lse_ref, qn_ref, kn_ref = refs[dil:]
    nb = length // A_SPAN
    qw = qw_ref[...] * (HEAD_DIM ** -0.5)
    kw = kw_ref[...]
    qi = lax.broadcasted_iota(jnp.int32, (A_SPAN, 2 * A_SPAN), 0)
    kj = lax.broadcasted_iota(jnp.int32, (A_SPAN, 2 * A_SPAN), 1)
    dist = qi + A_SPAN - kj
    band = (dist >= 0) & (dist <= A_SPAN)
    lane = lax.broadcasted_iota(jnp.int32, (A_SPAN, LANES), 1)
    vcol = 2 * A_GROUP_WIDTH

    def block(r, row0, first):
        qkv_ref = qkv_refs[r]
        lse_tile = jnp.zeros((A_SPAN, LANES), F32)
        for hd in range(A_HEADS_PER_GROUP):
            c0 = hd * HEAD_DIM
            qb = qn_ref[pl.ds(row0, A_SPAN), c0:c0 + HEAD_DIM]
            if first:
                kk = kn_ref[0:A_SPAN, c0:c0 + HEAD_DIM]
                vv = qkv_ref[0, 0:A_SPAN, vcol + c0:vcol + c0 + HEAD_DIM]
                mask = band[:, A_SPAN:]
            else:
                k0 = pl.multiple_of(row0 - A_SPAN, A_SPAN)
                kk = kn_ref[pl.ds(k0, 2 * A_SPAN), c0:c0 + HEAD_DIM]
                vv = qkv_ref[0, pl.ds(k0, 2 * A_SPAN), vcol + c0:vcol + c0 + HEAD_DIM]
                mask = band
            s = lax.dot_general(qb, kk, _NT, preferred_element_type=F32)
            s = jnp.where(mask, s, -jnp.inf)
            m = jnp.max(s, axis=-1, keepdims=True)
            p = jnp.exp(s - m)
            den = jnp.sum(p, axis=-1, keepdims=True)
            o = jnp.dot(p.astype(BF16), vv, preferred_element_type=F32) / den
            oc = r * A_GROUP_WIDTH + c0
            o_ref[0, pl.ds(row0, A_SPAN), oc:oc + HEAD_DIM] = o.astype(BF16)
            lse_tile = jnp.where(lane == hd, m + jnp.log(den), lse_tile)
        lse_ref[0, pl.ds(row0, A_SPAN), r * LANES:(r + 1) * LANES] = lse_tile

    for r in range(dil):
        qkv_ref = qkv_refs[r]
        for hd in range(A_HEADS_PER_GROUP):
            c0 = hd * HEAD_DIM
            q = qkv_ref[0, :, c0:c0 + HEAD_DIM].astype(F32)
            k = qkv_ref[0, :, A_GROUP_WIDTH + c0:A_GROUP_WIDTH + c0 + HEAD_DIM].astype(F32)
            qn = q * lax.rsqrt(jnp.mean(q * q, axis=-1, keepdims=True) + NORM_EPS) * qw
            kn = k * lax.rsqrt(jnp.mean(k * k, axis=-1, keepdims=True) + NORM_EPS) * kw
            qn_ref[:, c0:c0 + HEAD_DIM] = qn.astype(BF16)
            kn_ref[:, c0:c0 + HEAD_DIM] = kn.astype(BF16)
        block(r, 0, True)
        if nb > 1:
            def body(i, carry, r=r):
                block(r, pl.multiple_of(i * A_SPAN, A_SPAN), False)
                return carry
            lax.fori_loop(1, nb, body, 0)


def _attention_group(proj, g, qw, kw):
    bsz, s, ncols = proj.shape
    _, dil = A_GROUPS[g]
    length = s // dil
    blocks_per_row = ncols // A_GROUP_COLS
    col_blk = ATTN_COL0 // A_GROUP_COLS + g
    view = proj.reshape(bsz, length, dil * ncols)
    in_specs = [pl.BlockSpec((1, length, A_GROUP_COLS),
                             functools.partial(lambda b, r: (b, 0, r * blocks_per_row + col_blk), r=r))
                for r in range(dil)]
    in_specs += [pl.BlockSpec((1, HEAD_DIM), lambda b: (0, 0))] * 2
    o, lse = pl.pallas_call(
        functools.partial(_attn_kernel, dil=dil, length=length),
        grid=(bsz,),
        in_specs=in_specs,
        out_specs=[pl.BlockSpec((1, length, dil * A_GROUP_WIDTH), lambda b: (b, 0, 0)),
                   pl.BlockSpec((1, length, dil * LANES), lambda b: (b, 0, 0))],
        out_shape=[jax.ShapeDtypeStruct((bsz, length, dil * A_GROUP_WIDTH), BF16),
                   jax.ShapeDtypeStruct((bsz, length, dil * LANES), F32)],
        scratch_shapes=[pltpu.VMEM((length, A_GROUP_WIDTH), BF16),
                        pltpu.VMEM((length, A_GROUP_WIDTH), BF16)],
        compiler_params=_cparams(("arbitrary",)),
        name=f"attn{g}",
    )(*([view] * dil), qw.reshape(1, HEAD_DIM), kw.reshape(1, HEAD_DIM))
    return o.reshape(bsz, s, A_GROUP_WIDTH), lse.reshape(bsz, s, LANES)


def _split3_bf16(x):
    hi = x.astype(BF16)
    r1 = x - hi.astype(F32)
    mid = r1.astype(BF16)
    lo = (r1 - mid.astype(F32)).astype(BF16)
    return hi, mid, lo


def _bdot(a, b):
    return jnp.einsum("nik,nkj->nij", a, b, preferred_element_type=F32)


def _gdn_kernel(proj_ref, ba_ref, convw_ref, alog_ref, dtb_ref, onw_ref, y_ref,
                u_ref, w_ref, qd_ref, ke_ref, intra_ref, egl_ref, st_ref, *, hb, seq):
    c = B_CHUNK
    nc = seq // c
    row = lax.broadcasted_iota(jnp.int32, (seq, LANES), 0)
    pos = row & (c - 1)

    ba = ba_ref[0]
    beta_all = _sigmoid(ba)
    xs = ba + dtb_ref[0]
    softplus = jnp.maximum(xs, 0.0) + jnp.log(1.0 + jnp.exp(-jnp.abs(xs)))
    g_all = -jnp.exp(alog_ref[0]) * softplus
    gc = g_all
    for sh in (1, 2, 4, 8, 16, 32):
        gc = gc + jnp.where(pos >= sh, pltpu.roll(gc, sh, axis=0), 0.0)
    gc3 = gc.reshape(nc, c, LANES)
    glast = jnp.broadcast_to(gc3[:, c - 1:c, :], (nc, c, LANES)).reshape(seq, LANES)
    e_gc = jnp.exp(gc)
    e_rem = jnp.exp(glast - gc)
    e_gl = jnp.exp(glast)

    ii = lax.broadcasted_iota(jnp.int32, (c, c), 0)
    jj = lax.broadcasted_iota(jnp.int32, (c, c), 1)
    causal = (ii >= jj)[None]
    strict = (ii > jj)[None]
    strict_f = (ii > jj).astype(F32)[None]
    tril3 = jnp.broadcast_to(
        jnp.concatenate([(ii >= jj).astype(BF16)] * 3, axis=1)[None], (nc, c, 3 * c))

    def conv_silu(col0):
        x = proj_ref[0, :, col0:col0 + HEAD_DIM].astype(F32)
        cw = convw_ref[:, col0:col0 + HEAD_DIM]
        acc = x * cw[B_CONV - 1:B_CONV]
        for sh in range(1, B_CONV):
            xs_ = jnp.where(row >= sh, pltpu.roll(x, sh, axis=0), 0.0)
            acc = acc + xs_ * cw[B_CONV - 1 - sh:B_CONV - sh]
        return _silu(acc)

    for j in range(hb):
        c0 = j * B_HEAD_COLS
        q = conv_silu(c0)
        k = conv_silu(c0 + HEAD_DIM)
        v = conv_silu(c0 + 2 * HEAD_DIM)
        q = q * lax.rsqrt(jnp.sum(q * q, axis=-1, keepdims=True) + NORM_EPS) * (HEAD_DIM ** -0.5)
        k = k * lax.rsqrt(jnp.sum(k * k, axis=-1, keepdims=True) + NORM_EPS)
        a_lane = GDN_A_LANE + j
        beta = beta_all[:, j:j + 1]
        g_col = g_all[:, a_lane:a_lane + 1]
        egc = e_gc[:, a_lane:a_lane + 1]
        kb = k * beta
        k3 = k.astype(BF16).reshape(nc, c, HEAD_DIM)
        kb3 = kb.astype(BF16).reshape(nc, c, HEAD_DIM)
        q3 = q.astype(BF16).reshape(nc, c, HEAD_DIM)
        kk = jnp.einsum("nik,njk->nij", kb3, k3, preferred_element_type=F32)
        qk = jnp.einsum("nik,njk->nij", q3, k3, preferred_element_type=F32)
        bm = g_col.reshape(nc, c, 1) * strict_f
        diff = _bdot(tril3, jnp.concatenate(_split3_bf16(bm), axis=1))
        decay = jnp.exp(diff)
        lower = jnp.where(strict, kk * decay, 0.0)
        intra = jnp.where(causal, qk * decay, 0.0)
        m_pow = -lower
        n_acc = m_pow
        for _ in range(5):
            mb = m_pow.astype(BF16)
            m_pow = _bdot(mb, mb)
            n_acc = n_acc + m_pow + _bdot(n_acc.astype(BF16), m_pow.astype(BF16))
        rhs = jnp.concatenate([v * beta, kb * egc], axis=-1).reshape(nc, c, 2 * HEAD_DIM)
        sol = rhs + _bdot(n_acc.astype(BF16), rhs.astype(BF16))
        u_ref[j] = sol[:, :, :HEAD_DIM].reshape(seq, HEAD_DIM)
        w_ref[j] = sol[:, :, HEAD_DIM:].reshape(seq, HEAD_DIM).astype(BF16)
        qd_ref[j] = (q * egc).astype(BF16)
        ke_ref[j] = (k * e_rem[:, a_lane:a_lane + 1]).astype(BF16)
        intra_ref[j] = intra.reshape(seq, c).astype(BF16)
        egl_col = e_gl[:, a_lane:a_lane + 1].reshape(nc, c, 1)[:, 0, :]
        egl_ref[j] = jnp.broadcast_to(egl_col, (nc, LANES))

    st_ref[...] = jnp.zeros_like(st_ref)
    onw = onw_ref[...]

    def body(n, carry):
        r0 = pl.multiple_of(n * c, c)
        for j in range(hb):
            state = st_ref[j]
            wq = jnp.concatenate([w_ref[j, pl.ds(r0, c), :], qd_ref[j, pl.ds(r0, c), :]], axis=0)
            ws = jnp.dot(wq, state.astype(BF16), preferred_element_type=F32)
            v_new = (u_ref[j, pl.ds(r0, c), :] - ws[:c]).astype(BF16)
            o = ws[c:] + jnp.dot(intra_ref[j, pl.ds(r0, c), :], v_new, preferred_element_type=F32)
            kv = lax.dot_general(ke_ref[j, pl.ds(r0, c), :], v_new, _TN, preferred_element_type=F32)
            st_ref[j] = state * egl_ref[j, pl.ds(n, 1), :] + kv
            zc = j * B_HEAD_COLS + 3 * HEAD_DIM
            z = proj_ref[0, pl.ds(r0, c), zc:zc + HEAD_DIM].astype(F32)
            on = o * lax.rsqrt(jnp.mean(o * o, axis=-1, keepdims=True) + NORM_EPS) * onw
            y_ref[0, pl.ds(r0, c), j * HEAD_DIM:(j + 1) * HEAD_DIM] = (on * _silu(z)).astype(BF16)
        return carry

    lax.fori_loop(0, nc, body, 0)


def _gdn(proj, ba, convw, alog, dtb, onw, hb=GDN_HEADS_PER_STEP):
    bsz, s, _ = proj.shape
    ng = B_HEADS // hb
    nc = s // B_CHUNK
    return pl.pallas_call(
        functools.partial(_gdn_kernel, hb=hb, seq=s),
        grid=(bsz, ng),
        in_specs=[pl.BlockSpec((1, s, hb * B_HEAD_COLS), lambda b, g: (b, 0, g)),
                  pl.BlockSpec((1, s, LANES), lambda b, g: (b, 0, g)),
                  pl.BlockSpec((B_CONV, hb * B_HEAD_COLS), lambda b, g: (0, g)),
                  pl.BlockSpec((1, 1, LANES), lambda b, g: (g, 0, 0)),
                  pl.BlockSpec((1, 1, LANES), lambda b, g: (g, 0, 0)),
                  pl.BlockSpec((1, HEAD_DIM), lambda b, g: (0, 0))],
        out_specs=pl.BlockSpec((1, s, hb * HEAD_DIM), lambda b, g: (b, 0, g)),
        out_shape=jax.ShapeDtypeStruct((bsz, s, B_HEADS * HEAD_DIM), BF16),
        scratch_shapes=[pltpu.VMEM((hb, s, HEAD_DIM), F32),
                        pltpu.VMEM((hb, s, HEAD_DIM), BF16),
                        pltpu.VMEM((hb, s, HEAD_DIM), BF16),
                        pltpu.VMEM((hb, s, HEAD_DIM), BF16),
                        pltpu.VMEM((hb, s, B_CHUNK), BF16),
                        pltpu.VMEM((hb, nc, LANES), F32),
                        pltpu.VMEM((hb, HEAD_DIM, HEAD_DIM), F32)],
        compiler_params=_cparams(("arbitrary", "arbitrary")),
        name="gdn",
    )(proj, ba, convw, alog, dtb, onw.reshape(1, HEAD_DIM))


def _mix_kernel(o0_ref, o1_ref, o2_ref, l0_ref, l1_ref, l2_ref, yb_ref, ga_ref, gb_ref, x_ref,
                mod_ref, wa_ref, wb_ref, wo_ref, n2_ref, wr_ref, br_ref,
                x1_ref, h2_ref, cw_ref):
    tm = x_ref.shape[1]
    o_refs = (o0_ref, o1_ref, o2_ref)
    ls = [l0_ref[0], l1_ref[0], l2_ref[0]]
    mx = jnp.maximum(jnp.maximum(ls[0], ls[1]), ls[2])
    es = [jnp.exp(l - mx) for l in ls]
    den = es[0] + es[1] + es[2]
    ws = [e / den for e in es]
    heads = []
    for hd in range(A_HEADS_PER_GROUP):
        c0 = hd * HEAD_DIM
        acc = ws[0][:, hd:hd + 1] * o_refs[0][0, :, c0:c0 + HEAD_DIM].astype(F32)
        for g in (1, 2):
            acc = acc + ws[g][:, hd:hd + 1] * o_refs[g][0, :, c0:c0 + HEAD_DIM].astype(F32)
        heads.append(acc)
    ya = jnp.concatenate(heads, axis=1).astype(BF16)
    a = jnp.dot(ya, wa_ref[...], preferred_element_type=F32)
    bm = jnp.dot(yb_ref[0], wb_ref[...], preferred_element_type=F32)
    merged = _sigmoid(ga_ref[0].astype(F32)) * a + _sigmoid(gb_ref[0].astype(F32)) * bm
    mix = jnp.dot(merged.astype(BF16), wo_ref[...], preferred_element_type=F32)
    m = mod_ref[0]
    x1 = x_ref[0] + m[2:3] * mix
    x1_ref[0] = x1
    h2 = _modulated_norm(x1, n2_ref[...], m[4:5], m[3:4])
    h2_ref[0] = h2.astype(BF16)

    lt = lax.dot_general(wr_ref[...], h2, _NT, preferred_element_type=F32,
                         precision=lax.Precision.HIGHEST) + br_ref[...]
    gl = [lt[i:i + 1] for i in range(MOE_GROUPS)]
    gm = jnp.maximum(jnp.maximum(gl[0], gl[1]), jnp.maximum(gl[2], gl[3]))
    gsum = sum(jnp.exp(x - gm) for x in gl)
    p_top = 1.0 / gsum
    sel, taken = [], jnp.zeros(gm.shape, jnp.bool_)
    for i in range(MOE_GROUPS):
        s_i = jnp.logical_and(gl[i] == gm, jnp.logical_not(taken))
        taken = jnp.logical_or(taken, s_i)
        sel.append(s_i)
    ig = []
    for e in range(MOE_EXPERTS_PER_GROUP):
        acc = jnp.zeros(gm.shape, F32)
        for g in range(MOE_GROUPS):
            r = ROUTER_EXPERT_ROW0 + g * MOE_EXPERTS_PER_GROUP + e
            acc = jnp.where(sel[g], lt[r:r + 1], acc)
        ig.append(acc)
    t1 = jnp.maximum(jnp.maximum(ig[0], ig[1]), jnp.maximum(ig[2], ig[3]))
    is1, taken = [], jnp.zeros(gm.shape, jnp.bool_)
    for e in range(MOE_EXPERTS_PER_GROUP):
        s_e = jnp.logical_and(ig[e] == t1, jnp.logical_not(taken))
        taken = jnp.logical_or(taken, s_e)
        is1.append(s_e)
    rest = [jnp.where(is1[e], -jnp.inf, ig[e]) for e in range(MOE_EXPERTS_PER_GROUP)]
    t2 = jnp.maximum(jnp.maximum(rest[0], rest[1]), jnp.maximum(rest[2], rest[3]))
    is2, taken = [], jnp.zeros(gm.shape, jnp.bool_)
    for e in range(MOE_EXPERTS_PER_GROUP):
        s_e = jnp.logical_and(rest[e] == t2, jnp.logical_not(taken))
        taken = jnp.logical_or(taken, s_e)
        is2.append(s_e)
    e2 = jnp.exp(t2 - t1)
    w1 = p_top / (1.0 + e2)
    w2 = p_top * e2 / (1.0 + e2)
    wt = [jnp.where(is1[e], w1, 0.0) + jnp.where(is2[e], w2, 0.0) for e in range(MOE_EXPERTS_PER_GROUP)]
    rowi = lax.broadcasted_iota(jnp.int32, (LANES, tm), 0)
    for g in range(MOE_GROUPS):
        slab = jnp.zeros((LANES, tm), F32)
        for e in range(MOE_EXPERTS_PER_GROUP):
            val = jnp.where(sel[g], wt[e], 0.0)
            slab = jnp.where(rowi == e, jnp.broadcast_to(val, (LANES, tm)), slab)
        cw_ref[g, 0] = slab.T


def _mix(o_list, lse_list, yb, proj, x, mod, wa, wb, wo, n2w, wr, br, tm=512):
    bsz, s, d = x.shape
    tok = lambda b, i: (b, i, 0)
    full = lambda b, i: (0, 0)
    gate_blk = GATE_COL0 // d
    in_specs = ([pl.BlockSpec((1, tm, A_GROUP_WIDTH), tok)] * 3
                + [pl.BlockSpec((1, tm, LANES), tok)] * 3
                + [pl.BlockSpec((1, tm, d), tok),
                   pl.BlockSpec((1, tm, d), lambda b, i: (b, i, gate_blk)),
                   pl.BlockSpec((1, tm, d), lambda b, i: (b, i, gate_blk + 1)),
                   pl.BlockSpec((1, tm, d), tok),
                   pl.BlockSpec((1, N_MOD, d), lambda b, i: (b, 0, 0)),
                   pl.BlockSpec(wa.shape, full),
                   pl.BlockSpec(wb.shape, full),
                   pl.BlockSpec(wo.shape, full),
                   pl.BlockSpec((1, d), full),
                   pl.BlockSpec(wr.shape, full),
                   pl.BlockSpec(br.shape, full)])
    return pl.pallas_call(
        _mix_kernel,
        grid=(bsz, s // tm),
        in_specs=in_specs,
        out_specs=[pl.BlockSpec((1, tm, d), tok),
                   pl.BlockSpec((1, tm, d), tok),
                   pl.BlockSpec((MOE_GROUPS, 1, tm, LANES), lambda b, i: (0, b, i, 0))],
        out_shape=[jax.ShapeDtypeStruct((bsz, s, d), F32),
                   jax.ShapeDtypeStruct((bsz, s, d), BF16),
                   jax.ShapeDtypeStruct((MOE_GROUPS, bsz, s, LANES), F32)],
        compiler_params=_cparams(("arbitrary", "arbitrary")),
        name="mix",
    )(*o_list, *lse_list, yb, proj, proj, x, mod, wa, wb, wo, n2w.reshape(1, d), wr, br)


def _moe_kernel(h2_ref, cw_ref, wg_ref, wu_ref, wd_ref, x1_ref, mod_ref, out_ref, acc_ref):
    g = pl.program_id(1)
    h = h2_ref[...]
    hid = _silu(jnp.dot(h, wg_ref[0], preferred_element_type=F32)) * jnp.dot(
        h, wu_ref[0], preferred_element_type=F32)
    cw = cw_ref[0]
    parts = [hid[:, e * MOE_D_FF:(e + 1) * MOE_D_FF] * cw[:, e:e + 1] for e in range(MOE_EXPERTS_PER_GROUP)]
    contrib = jnp.dot(jnp.concatenate(parts, axis=1).astype(BF16), wd_ref[0], preferred_element_type=F32)

    @pl.when(g == 0)
    def _():
        acc_ref[...] = contrib

    @pl.when(g > 0)
    def _():
        acc_ref[...] += contrib

    @pl.when(g == MOE_GROUPS - 1)
    def _():
        out_ref[...] = x1_ref[...] + mod_ref[0][5:6] * acc_ref[...]


def _moe(h2, cw, wg, wu, wd, x1, mod, seq, tm=1024):
    t, d = h2.shape
    ff = wg.shape[2]
    return pl.pallas_call(
        _moe_kernel,
        grid=(t // tm, MOE_GROUPS),
        in_specs=[pl.BlockSpec((tm, d), lambda i, g: (i, 0)),
                  pl.BlockSpec((1, tm, LANES), lambda i, g: (g, i, 0)),
                  pl.BlockSpec((1, d, ff), lambda i, g: (g, 0, 0)),
                  pl.BlockSpec((1, d, ff), lambda i, g: (g, 0, 0)),
                  pl.BlockSpec((1, ff, d), lambda i, g: (g, 0, 0)),
                  pl.BlockSpec((tm, d), lambda i, g: (i, 0)),
                  pl.BlockSpec((1, N_MOD, d), lambda i, g: ((i * tm) // seq, 0, 0))],
        out_specs=pl.BlockSpec((tm, d), lambda i, g: (i, 0)),
        out_shape=jax.ShapeDtypeStruct((t, d), F32),
        scratch_shapes=[pltpu.VMEM((tm, d), F32)],
        compiler_params=_cparams(("arbitrary", "arbitrary")),
        name="moe",
    )(h2, cw, wg, wu, wd, x1, mod)


def _prep_in_weights(w_in, conv_w, a_log, dt_bias, hb):
    d = w_in.shape[0]
    aw = 3 * A_GROUP_WIDTH
    bw = B_HEADS * HEAD_DIM
    offs = np.cumsum([0, aw, aw, aw, bw, bw, bw, bw, B_HEADS, B_HEADS, d, d])
    qa, ka, va, qb, kb, vb, zb, beta, a_raw, ga, gb = [w_in[:, offs[i]:offs[i + 1]] for i in range(11)]
    cols, conv_cols = [], []
    for h in range(B_HEADS):
        sl = slice(h * HEAD_DIM, (h + 1) * HEAD_DIM)
        cols += [qb[:, sl], kb[:, sl], vb[:, sl], zb[:, sl]]
        conv_cols += [conv_w[:, sl], conv_w[:, bw + h * HEAD_DIM:bw + (h + 1) * HEAD_DIM],
                      conv_w[:, 2 * bw + h * HEAD_DIM:2 * bw + (h + 1) * HEAD_DIM],
                      jnp.zeros((B_CONV, HEAD_DIM), conv_w.dtype)]
    cols += [ga, gb]
    for g in range(len(A_GROUPS)):
        sl = slice(g * A_GROUP_WIDTH, (g + 1) * A_GROUP_WIDTH)
        cols += [qa[:, sl], ka[:, sl], va[:, sl]]
    w1 = jnp.concatenate(cols, axis=1).astype(BF16)
    convw = jnp.concatenate(conv_cols, axis=1)
    ng = B_HEADS // hb
    pad = jnp.zeros((d, LANES - GDN_A_LANE - hb), w_in.dtype)
    gap = jnp.zeros((d, GDN_A_LANE - hb), w_in.dtype)
    ba_cols = []
    vec_pad = lambda v: jnp.concatenate([jnp.zeros((GDN_A_LANE,), F32), v, jnp.zeros((LANES - GDN_A_LANE - hb,), F32)])
    alog, dtb = [], []
    for gi in range(ng):
        sl = slice(gi * hb, (gi + 1) * hb)
        ba_cols += [beta[:, sl], gap, a_raw[:, sl], pad]
        alog.append(vec_pad(a_log[sl].astype(F32)))
        dtb.append(vec_pad(dt_bias[sl].astype(F32)))
    wba = jnp.concatenate(ba_cols, axis=1).astype(BF16)
    return w1, wba, convw, jnp.stack(alog)[:, None, :], jnp.stack(dtb)[:, None, :]


def _prep_router(w_rg, b_rg, w_re, b_re):
    d = w_rg.shape[0]
    n_exp = MOE_GROUPS * MOE_EXPERTS_PER_GROUP
    wr = jnp.concatenate([w_rg.T, jnp.zeros((ROUTER_EXPERT_ROW0 - MOE_GROUPS, d), F32), w_re.T,
                          jnp.zeros((ROUTER_ROWS - ROUTER_EXPERT_ROW0 - n_exp, d), F32)], axis=0)
    br = jnp.concatenate([b_rg, jnp.zeros((ROUTER_EXPERT_ROW0 - MOE_GROUPS,), F32), b_re.reshape(-1),
                          jnp.zeros((ROUTER_ROWS - ROUTER_EXPERT_ROW0 - n_exp,), F32)])
    return wr.astype(F32), br.astype(F32)[:, None]


def _prep_experts(w_gate, w_up, w_down):
    ne, d, ff = w_gate.shape
    per_group = lambda w: (w.reshape(MOE_GROUPS, MOE_EXPERTS_PER_GROUP, d, ff).transpose(0, 2, 1, 3)
                           .reshape(MOE_GROUPS, d, MOE_EXPERTS_PER_GROUP * ff).astype(BF16))
    wd = w_down.reshape(MOE_GROUPS, MOE_EXPERTS_PER_GROUP * ff, d).astype(BF16)
    return per_group(w_gate), per_group(w_up), wd


def kernel(x, c, w_ada, b_ada, norm1_w, w_in, conv_w, a_q_norm_w, a_k_norm_w, b_A_log, b_dt_bias, b_out_norm_w, w_branch_a, w_branch_b, w_o, norm2_w, w_router_group, b_router_group, w_router_expert, b_router_expert, w_exp_gate, w_exp_up, w_exp_down):
    bsz, s, d = x.shape
    hb = GDN_HEADS_PER_STEP
    for i in range(w_ada.shape[0]):
        mod = _ada(c, w_ada[i], b_ada[i]).reshape(bsz, N_MOD, d)
        w1, wba, convw, alog, dtb = _prep_in_weights(w_in[i], conv_w[i], b_A_log[i], b_dt_bias[i], hb)
        proj, ba = _inproj(x, mod, norm1_w[i], w1, wba)
        o_list, lse_list = [], []
        for g in range(len(A_GROUPS)):
            o_g, lse_g = _attention_group(proj, g, a_q_norm_w[i], a_k_norm_w[i])
            o_list.append(o_g)
            lse_list.append(lse_g)
        yb = _gdn(proj, ba, convw, alog, dtb, b_out_norm_w[i], hb)
        wr, br = _prep_router(w_router_group[i], b_router_group[i], w_router_expert[i], b_router_expert[i])
        x1, h2, cw = _mix(o_list, lse_list, yb, proj, x, mod, w_branch_a[i].astype(BF16),
                          w_branch_b[i].astype(BF16), w_o[i].astype(BF16), norm2_w[i], wr, br)
        wg, wu, wd = _prep_experts(w_exp_gate[i], w_exp_up[i], w_exp_down[i])
        out = _moe(h2.reshape(bsz * s, d), cw.reshape(MOE_GROUPS, bsz * s, LANES), wg, wu, wd,
                   x1.reshape(bsz * s, d), mod, s)
        x = out.reshape(bsz, s, d)
    return x
```

```python
import functools

import jax
import jax.numpy as jnp
import numpy as np
from jax import lax
from jax.experimental import pallas as pl
from jax.experimental.pallas import tpu as pltpu

F32 = jnp.float32
BF16 = jnp.bfloat16

NORM_EPS = 1e-6
N_MOD = 6

A_GROUPS = ((128, 1), (512, 4), (2048, 16))
A_HEADS_PER_GROUP = 4
HEAD_DIM = 128
A_GROUP_WIDTH = A_HEADS_PER_GROUP * HEAD_DIM
A_GROUP_COLS = 3 * A_GROUP_WIDTH
A_SPAN = 128

B_HEADS = 8
B_CONV = 4
B_CHUNK = 64
B_HEAD_COLS = 4 * HEAD_DIM
GDN_HEADS_PER_STEP = 4
GDN_A_LANE = 8

MOE_GROUPS = 4
MOE_EXPERTS_PER_GROUP = 4
MOE_D_FF = 256
ROUTER_ROWS = 32
ROUTER_EXPERT_ROW0 = 8

GDN_COLS = B_HEADS * B_HEAD_COLS
GATE_COL0 = GDN_COLS
ATTN_COL0 = GDN_COLS + 2048
PROJ_COLS = ATTN_COL0 + 3 * A_GROUP_COLS

LANES = 128
VMEM_LIMIT_BYTES = 56 * 1024 * 1024

_NT = (((1,), (1,)), ((), ()))
_TN = (((0,), (0,)), ((), ()))


def _sigmoid(x):
    return 1.0 / (1.0 + jnp.exp(-x))


def _silu(x):
    return x * _sigmoid(x)


def _cparams(sem):
    return pltpu.CompilerParams(dimension_semantics=sem, vmem_limit_bytes=VMEM_LIMIT_BYTES)


def _ada_kernel(c_ref, w_ref, b_ref, o_ref):
    c = c_ref[...]
    o_ref[...] = jnp.dot(_silu(c), w_ref[...], preferred_element_type=F32,
                         precision=lax.Precision.HIGHEST) + b_ref[...]


def _ada(c, w_ada, b_ada):
    bsz, d = c.shape
    n = w_ada.shape[1]
    tn = 1536
    return pl.pallas_call(
        _ada_kernel,
        grid=(n // tn,),
        in_specs=[pl.BlockSpec((bsz, d), lambda j: (0, 0)),
                  pl.BlockSpec((d, tn), lambda j: (0, j)),
                  pl.BlockSpec((1, tn), lambda j: (0, j))],
        out_specs=pl.BlockSpec((bsz, tn), lambda j: (0, j)),
        out_shape=jax.ShapeDtypeStruct((bsz, n), F32),
        compiler_params=_cparams(("arbitrary",)),
        name="ada",
    )(c, w_ada, b_ada.reshape(1, n))


def _modulated_norm(x, norm_w, scale, shift):
    y = x * lax.rsqrt(jnp.mean(x * x, axis=-1, keepdims=True) + NORM_EPS)
    return (y * norm_w) * (1.0 + scale) + shift


def _inproj_kernel(x_ref, mod_ref, nw_ref, w_ref, wba_ref, proj_ref, ba_ref, h_ref):
    @pl.when(pl.program_id(2) == 0)
    def _():
        m = mod_ref[0]
        h = _modulated_norm(x_ref[0], nw_ref[...], m[1:2], m[0:1]).astype(BF16)
        h_ref[...] = h
        ba_ref[0] = jnp.dot(h, wba_ref[...], preferred_element_type=F32)

    proj_ref[0] = jnp.dot(h_ref[...], w_ref[...], preferred_element_type=F32).astype(BF16)


def _inproj(x, mod, norm_w, w1, wba, tm=1024, tn=1536):
    bsz, s, d = x.shape
    n = w1.shape[1]
    nba = wba.shape[1]
    return pl.pallas_call(
        _inproj_kernel,
        grid=(bsz, s // tm, n // tn),
        in_specs=[pl.BlockSpec((1, tm, d), lambda b, i, j: (b, i, 0)),
                  pl.BlockSpec((1, N_MOD, d), lambda b, i, j: (b, 0, 0)),
                  pl.BlockSpec((1, d), lambda b, i, j: (0, 0)),
                  pl.BlockSpec((d, tn), lambda b, i, j: (0, j)),
                  pl.BlockSpec((d, nba), lambda b, i, j: (0, 0))],
        out_specs=[pl.BlockSpec((1, tm, tn), lambda b, i, j: (b, i, j)),
                   pl.BlockSpec((1, tm, nba), lambda b, i, j: (b, i, 0))],
        out_shape=[jax.ShapeDtypeStruct((bsz, s, n), BF16),
                   jax.ShapeDtypeStruct((bsz, s, nba), F32)],
        scratch_shapes=[pltpu.VMEM((tm, d), BF16)],
        compiler_params=_cparams(("arbitrary", "arbitrary", "arbitrary")),
        name="inproj",
    )(x, mod, norm_w.reshape(1, d), w1, wba)


def _attn_kernel(qkv_ref, qw_ref, kw_ref, o_ref, lse_ref, qn_ref, kn_ref, vn_ref, *rm_refs, dil, length):
    seq = dil * length
    nb = length // A_SPAN
    if dil > 1:
        stage_ref, orm_ref, lrm_ref = rm_refs
    qw = qw_ref[...] * (HEAD_DIM ** -0.5)
    kw = kw_ref[...]

    def to_residue_major(x, dst_ref, row_off, c0):
        if dil == 1:
            dst_ref[row_off:row_off + seq, c0:c0 + HEAD_DIM] = x.astype(BF16)
        else:
            stage_ref[...] = x
            for r in range(dil):
                lo = row_off + r * length
                dst_ref[lo:lo + length, c0:c0 + HEAD_DIM] = (
                    stage_ref[pl.ds(r, length, stride=dil), :].astype(BF16))

    ones_sq = jnp.ones((HEAD_DIM, HEAD_DIM), BF16)

    def mean_sq(x):
        return jnp.dot((x * x).astype(BF16), ones_sq, preferred_element_type=F32) * (1.0 / HEAD_DIM)

    kn_ref[0:A_SPAN, :] = jnp.zeros((A_SPAN, A_GROUP_WIDTH), BF16)
    vn_ref[0:A_SPAN, :] = jnp.zeros((A_SPAN, A_GROUP_WIDTH), BF16)
    for hd in range(A_HEADS_PER_GROUP):
        c0 = hd * HEAD_DIM
        q = qkv_ref[0, :, c0:c0 + HEAD_DIM].astype(F32)
        k = qkv_ref[0, :, A_GROUP_WIDTH + c0:A_GROUP_WIDTH + c0 + HEAD_DIM].astype(F32)
        v = qkv_ref[0, :, 2 * A_GROUP_WIDTH + c0:2 * A_GROUP_WIDTH + c0 + HEAD_DIM].astype(F32)
        qn = q * lax.rsqrt(mean_sq(q) + NORM_EPS) * qw
        kn = k * lax.rsqrt(mean_sq(k) + NORM_EPS) * kw
        to_residue_major(qn, qn_ref, 0, c0)
        to_residue_major(kn, kn_ref, A_SPAN, c0)
        to_residue_major(v, vn_ref, A_SPAN, c0)

    qi = lax.broadcasted_iota(jnp.int32, (A_SPAN, 2 * A_SPAN), 0)
    kj = lax.broadcasted_iota(jnp.int32, (A_SPAN, 2 * A_SPAN), 1)
    dist = qi + A_SPAN - kj
    band = (dist >= 0) & (dist <= A_SPAN)
    lane = lax.broadcasted_iota(jnp.int32, (A_SPAN, LANES), 1)

    def body(bi, carry):
        row0 = pl.multiple_of(bi * A_SPAN, A_SPAN)
        first_key = jnp.where((bi & (nb - 1)) == 0, A_SPAN, 0)
        mask = band & (kj >= first_key)
        lse_tile = jnp.zeros((A_SPAN, LANES), F32)
        heads = range(A_HEADS_PER_GROUP)
        cols = [slice(hd * HEAD_DIM, (hd + 1) * HEAD_DIM) for hd in heads]
        s = [lax.dot_general(qn_ref[pl.ds(row0, A_SPAN), cols[hd]], kn_ref[pl.ds(row0, 2 * A_SPAN), cols[hd]],
                             _NT, preferred_element_type=F32) for hd in heads]
        s = [jnp.where(mask, s[hd], -jnp.inf) for hd in heads]
        m = [jnp.max(s[hd], axis=-1, keepdims=True) for hd in heads]
        p = [jnp.exp(s[hd] - m[hd]) for hd in heads]
        den = [jnp.sum(p[hd], axis=-1, keepdims=True) for hd in heads]
        pv = [jnp.dot(p[hd].astype(BF16), vn_ref[pl.ds(row0, 2 * A_SPAN), cols[hd]],
                      preferred_element_type=F32) for hd in heads]
        for hd in heads:
            o = pv[hd] / den[hd]
            if dil == 1:
                o_ref[0, pl.ds(row0, A_SPAN), cols[hd]] = o.astype(BF16)
            else:
                orm_ref[pl.ds(row0, A_SPAN), cols[hd]] = o
            lse_tile = jnp.where(lane == hd, m[hd] + jnp.log(den[hd]), lse_tile)
        if dil == 1:
            lse_ref[0, pl.ds(row0, A_SPAN), :] = lse_tile
        else:
            lrm_ref[pl.ds(row0, A_SPAN), :] = lse_tile
        return carry

    lax.fori_loop(0, seq // A_SPAN, body, 0)

    if dil > 1:
        for hd in range(A_HEADS_PER_GROUP):
            c0 = hd * HEAD_DIM
            for r in range(dil):
                stage_ref[pl.ds(r, length, stride=dil), :] = orm_ref[r * length:(r + 1) * length, c0:c0 + HEAD_DIM]
            o_ref[0, :, c0:c0 + HEAD_DIM] = stage_ref[...].astype(BF16)
        for r in range(dil):
            stage_ref[pl.ds(r, length, stride=dil), :] = lrm_ref[r * length:(r + 1) * length, :]
        lse_ref[0] = stage_ref[...]


def _attention_group(proj, g, qw, kw):
    bsz, s, _ = proj.shape
    _, dil = A_GROUPS[g]
    length = s // dil
    col_blk = ATTN_COL0 // A_GROUP_COLS + g
    scratch = [pltpu.VMEM((s, A_GROUP_WIDTH), BF16),
               pltpu.VMEM((s + A_SPAN, A_GROUP_WIDTH), BF16),
               pltpu.VMEM((s + A_SPAN, A_GROUP_WIDTH), BF16)]
    if dil > 1:
        scratch += [pltpu.VMEM((s, HEAD_DIM), F32),
                    pltpu.VMEM((s, A_GROUP_WIDTH), F32),
                    pltpu.VMEM((s, LANES), F32)]
    return pl.pallas_call(
        functools.partial(_attn_kernel, dil=dil, length=length),
        grid=(bsz,),
        in_specs=[pl.BlockSpec((1, s, A_GROUP_COLS), lambda b: (b, 0, col_blk)),
                  pl.BlockSpec((1, HEAD_DIM), lambda b: (0, 0)),
                  pl.BlockSpec((1, HEAD_DIM), lambda b: (0, 0))],
        out_specs=[pl.BlockSpec((1, s, A_GROUP_WIDTH), lambda b: (b, 0, 0)),
                   pl.BlockSpec((1, s, LANES), lambda b: (b, 0, 0))],
        out_shape=[jax.ShapeDtypeStruct((bsz, s, A_GROUP_WIDTH), BF16),
                   jax.ShapeDtypeStruct((bsz, s, LANES), F32)],
        scratch_shapes=scratch,
        compiler_params=_cparams(("arbitrary",)),
        name=f"attn{g}",
    )(proj, qw.reshape(1, HEAD_DIM), kw.reshape(1, HEAD_DIM))


def _split3_bf16(x):
    hi = x.astype(BF16)
    r1 = x - hi.astype(F32)
    mid = r1.astype(BF16)
    lo = (r1 - mid.astype(F32)).astype(BF16)
    return hi, mid, lo


def _bdot(a, b):
    return jnp.einsum("nik,nkj->nij", a, b, preferred_element_type=F32)


def _gdn_kernel(proj_ref, ba_ref, convw_ref, alog_ref, dtb_ref, onw_ref, y_ref,
                u_ref, w_ref, qd_ref, ket_ref, intra_ref, egl_ref, st_ref, *, hb, seq):
    c = B_CHUNK
    nc = seq // c
    row = lax.broadcasted_iota(jnp.int32, (seq, LANES), 0)

    ba = ba_ref[0]
    beta_all = _sigmoid(ba)
    xs = ba + dtb_ref[0]
    softplus = jnp.maximum(xs, 0.0) + jnp.log(1.0 + jnp.exp(-jnp.abs(xs)))
    g_all = -jnp.exp(alog_ref[0]) * softplus

    ii = lax.broadcasted_iota(jnp.int32, (c, c), 0)
    jj = lax.broadcasted_iota(jnp.int32, (c, c), 1)
    causal = (ii >= jj)[None]
    strict = (ii > jj)[None]
    strict_f = (ii > jj).astype(F32)[None]
    tril3 = jnp.broadcast_to(
        jnp.concatenate([(ii >= jj).astype(BF16)] * 3, axis=1)[None], (nc, c, 3 * c))
    ones_sq = jnp.ones((HEAD_DIM, HEAD_DIM), BF16)

    gc3 = _bdot(tril3, jnp.concatenate(_split3_bf16(g_all.reshape(nc, c, LANES)), axis=1))
    gc = gc3.reshape(seq, LANES)
    glast = jnp.broadcast_to(gc3[:, c - 1:c, :], (nc, c, LANES)).reshape(seq, LANES)
    e_gc = jnp.exp(gc)
    e_rem = jnp.exp(glast - gc)
    e_gl = jnp.exp(glast)

    def row_sumsq(x):
        return jnp.dot((x * x).astype(BF16), ones_sq, preferred_element_type=F32)

    def conv_silu(col0):
        x = proj_ref[0, :, col0:col0 + HEAD_DIM].astype(F32)
        cw = convw_ref[:, col0:col0 + HEAD_DIM]
        acc = x * cw[B_CONV - 1:B_CONV]
        for sh in range(1, B_CONV):
            xs_ = jnp.where(row >= sh, pltpu.roll(x, sh, axis=0), 0.0)
            acc = acc + xs_ * cw[B_CONV - 1 - sh:B_CONV - sh]
        return _silu(acc)

    for j in range(hb):
        c0 = j * B_HEAD_COLS
        q = conv_silu(c0)
        k = conv_silu(c0 + HEAD_DIM)
        v = conv_silu(c0 + 2 * HEAD_DIM)
        q = q * lax.rsqrt(row_sumsq(q) + NORM_EPS) * (HEAD_DIM ** -0.5)
        k = k * lax.rsqrt(row_sumsq(k) + NORM_EPS)
        a_lane = GDN_A_LANE + j
        beta = beta_all[:, j:j + 1]
        g_col = g_all[:, a_lane:a_lane + 1]
        egc = e_gc[:, a_lane:a_lane + 1]
        kb = k * beta
        k3 = k.astype(BF16).reshape(nc, c, HEAD_DIM)
        kb3 = kb.astype(BF16).reshape(nc, c, HEAD_DIM)
        q3 = q.astype(BF16).reshape(nc, c, HEAD_DIM)
        kk = jnp.einsum("nik,njk->nij", kb3, k3, preferred_element_type=F32)
        qk = jnp.einsum("nik,njk->nij", q3, k3, preferred_element_type=F32)
        bm = g_col.reshape(nc, c, 1) * strict_f
        diff = _bdot(tril3, jnp.concatenate(_split3_bf16(bm), axis=1))
        decay = jnp.exp(diff)
        lower = jnp.where(strict, kk * decay, 0.0)
        intra = jnp.where(causal, qk * decay, 0.0)
        m_pow = -lower
        n_acc = m_pow
        for _ in range(5):
            mb = m_pow.astype(BF16)
            m_pow = _bdot(mb, mb)
            n_acc = n_acc + m_pow + _bdot(n_acc.astype(BF16), m_pow.astype(BF16))
        rhs = jnp.concatenate([v * beta, kb * egc], axis=-1).reshape(nc, c, 2 * HEAD_DIM)
        sol = rhs + _bdot(n_acc.astype(BF16), rhs.astype(BF16))
        u_ref[j] = sol[:, :, :HEAD_DIM].reshape(seq, HEAD_DIM)
        w_ref[j] = sol[:, :, HEAD_DIM:].reshape(seq, HEAD_DIM).astype(BF16)
        qd_ref[j] = (q * egc).astype(BF16)
        ke3 = (k * e_rem[:, a_lane:a_lane + 1]).reshape(nc, c, HEAD_DIM)
        ket_ref[j] = jnp.swapaxes(ke3, 1, 2).reshape(nc * HEAD_DIM, c).astype(BF16)
        intra_ref[j] = intra.reshape(seq, c).astype(BF16)
        egl_col = e_gl[:, a_lane:a_lane + 1].reshape(nc, c, 1)[:, 0, :]
        egl_ref[j] = jnp.broadcast_to(egl_col, (nc, LANES))

    st_ref[...] = jnp.zeros_like(st_ref)
    onw = onw_ref[...]

    def body(n, carry):
        r0 = pl.multiple_of(n * c, c)
        k0 = pl.multiple_of(n * HEAD_DIM, HEAD_DIM)
        heads = range(hb)
        state = [st_ref[j] for j in heads]
        ws = [jnp.dot(jnp.concatenate([w_ref[j, pl.ds(r0, c), :], qd_ref[j, pl.ds(r0, c), :]], axis=0),
                      state[j].astype(BF16), preferred_element_type=F32) for j in heads]
        v_new = [(u_ref[j, pl.ds(r0, c), :] - ws[j][:c]).astype(BF16) for j in heads]
        kv = [jnp.dot(ket_ref[j, pl.ds(k0, HEAD_DIM), :], v_new[j], preferred_element_type=F32) for j in heads]
        o = [ws[j][c:] + jnp.dot(intra_ref[j, pl.ds(r0, c), :], v_new[j], preferred_element_type=F32)
             for j in heads]
        for j in heads:
            st_ref[j] = state[j] * egl_ref[j, pl.ds(n, 1), :] + kv[j]
        ss = [row_sumsq(o[j]) for j in heads]
        for j in heads:
            zc = j * B_HEAD_COLS + 3 * HEAD_DIM
            z = proj_ref[0, pl.ds(r0, c), zc:zc + HEAD_DIM].astype(F32)
            on = o[j] * lax.rsqrt(ss[j] * (1.0 / HEAD_DIM) + NORM_EPS) * onw
            y_ref[0, pl.ds(r0, c), j * HEAD_DIM:(j + 1) * HEAD_DIM] = (on * _silu(z)).astype(BF16)
        return carry

    lax.fori_loop(0, nc, body, 0)


def _gdn(proj, ba, convw, alog, dtb, onw, hb=GDN_HEADS_PER_STEP):
    bsz, s, _ = proj.shape
    ng = B_HEADS // hb
    nc = s // B_CHUNK
    return pl.pallas_call(
        functools.partial(_gdn_kernel, hb=hb, seq=s),
        grid=(bsz, ng),
        in_specs=[pl.BlockSpec((1, s, hb * B_HEAD_COLS), lambda b, g: (b, 0, g)),
                  pl.BlockSpec((1, s, LANES), lambda b, g: (b, 0, g)),
                  pl.BlockSpec((B_CONV, hb * B_HEAD_COLS), lambda b, g: (0, g)),
                  pl.BlockSpec((1, 1, LANES), lambda b, g: (g, 0, 0)),
                  pl.BlockSpec((1, 1, LANES), lambda b, g: (g, 0, 0)),
                  pl.BlockSpec((1, HEAD_DIM), lambda b, g: (0, 0))],
        out_specs=pl.BlockSpec((1, s, hb * HEAD_DIM), lambda b, g: (b, 0, g)),
        out_shape=jax.ShapeDtypeStruct((bsz, s, B_HEADS * HEAD_DIM), BF16),
        scratch_shapes=[pltpu.VMEM((hb, s, HEAD_DIM), F32),
                        pltpu.VMEM((hb, s, HEAD_DIM), BF16),
                        pltpu.VMEM((hb, s, HEAD_DIM), BF16),
                        pltpu.VMEM((hb, nc * HEAD_DIM, B_CHUNK), BF16),
                        pltpu.VMEM((hb, s, B_CHUNK), BF16),
                        pltpu.VMEM((hb, nc, LANES), F32),
                        pltpu.VMEM((hb, HEAD_DIM, HEAD_DIM), F32)],
        compiler_params=_cparams(("arbitrary", "arbitrary")),
        name="gdn",
    )(proj, ba, convw, alog, dtb, onw.reshape(1, HEAD_DIM))


def _mix_kernel(o0_ref, o1_ref, o2_ref, l0_ref, l1_ref, l2_ref, yb_ref, ga_ref, gb_ref, x_ref,
                mod_ref, wa_ref, wb_ref, wo_ref, n2_ref, wr_ref, br_ref,
                x1_ref, h2_ref, cw_ref):
    tm = x_ref.shape[1]
    o_refs = (o0_ref, o1_ref, o2_ref)
    ls = [l0_ref[0], l1_ref[0], l2_ref[0]]
    mx = jnp.maximum(jnp.maximum(ls[0], ls[1]), ls[2])
    es = [jnp.exp(l - mx) for l in ls]
    den = es[0] + es[1] + es[2]
    ws = [e / den for e in es]
    heads = []
    for hd in range(A_HEADS_PER_GROUP):
        c0 = hd * HEAD_DIM
        acc = ws[0][:, hd:hd + 1] * o_refs[0][0, :, c0:c0 + HEAD_DIM].astype(F32)
        for g in (1, 2):
            acc = acc + ws[g][:, hd:hd + 1] * o_refs[g][0, :, c0:c0 + HEAD_DIM].astype(F32)
        heads.append(acc)
    ya = jnp.concatenate(heads, axis=1).astype(BF16)
    a = jnp.dot(ya, wa_ref[...], preferred_element_type=F32)
    bm = jnp.dot(yb_ref[0], wb_ref[...], preferred_element_type=F32)
    merged = _sigmoid(ga_ref[0].astype(F32)) * a + _sigmoid(gb_ref[0].astype(F32)) * bm
    mix = jnp.dot(merged.astype(BF16), wo_ref[...], preferred_element_type=F32)
    m = mod_ref[0]
    x1 = x_ref[0] + m[2:3] * mix
    x1_ref[0] = x1
    h2 = _modulated_norm(x1, n2_ref[...], m[4:5], m[3:4])
    h2_ref[0] = h2.astype(BF16)

    lt = lax.dot_general(wr_ref[...], h2, _NT, preferred_element_type=F32,
                         precision=lax.Precision.HIGHEST) + br_ref[...]
    gl = [lt[i:i + 1] for i in range(MOE_GROUPS)]
    gm = jnp.maximum(jnp.maximum(gl[0], gl[1]), jnp.maximum(gl[2], gl[3]))
    gsum = sum(jnp.exp(x - gm) for x in gl)
    p_top = 1.0 / gsum
    sel, taken = [], jnp.zeros(gm.shape, jnp.bool_)
    for i in range(MOE_GROUPS):
        s_i = jnp.logical_and(gl[i] == gm, jnp.logical_not(taken))
        taken = jnp.logical_or(taken, s_i)
        sel.append(s_i)
    ig = []
    for e in range(MOE_EXPERTS_PER_GROUP):
        acc = jnp.zeros(gm.shape, F32)
        for g in range(MOE_GROUPS):
            r = ROUTER_EXPERT_ROW0 + g * MOE_EXPERTS_PER_GROUP + e
            acc = jnp.where(sel[g], lt[r:r + 1], acc)
        ig.append(acc)
    t1 = jnp.maximum(jnp.maximum(ig[0], ig[1]), jnp.maximum(ig[2], ig[3]))
    is1, taken = [], jnp.zeros(gm.shape, jnp.bool_)
    for e in range(MOE_EXPERTS_PER_GROUP):
        s_e = jnp.logical_and(ig[e] == t1, jnp.logical_not(taken))
        taken = jnp.logical_or(taken, s_e)
        is1.append(s_e)
    rest = [jnp.where(is1[e], -jnp.inf, ig[e]) for e in range(MOE_EXPERTS_PER_GROUP)]
    t2 = jnp.maximum(jnp.maximum(rest[0], rest[1]), jnp.maximum(rest[2], rest[3]))
    is2, taken = [], jnp.zeros(gm.shape, jnp.bool_)
    for e in range(MOE_EXPERTS_PER_GROUP):
        s_e = jnp.logical_and(rest[e] == t2, jnp.logical_not(taken))
        taken = jnp.logical_or(taken, s_e)
        is2.append(s_e)
    e2 = jnp.exp(t2 - t1)
    w1 = p_top / (1.0 + e2)
    w2 = p_top * e2 / (1.0 + e2)
    wt = [jnp.where(is1[e], w1, 0.0) + jnp.where(is2[e], w2, 0.0) for e in range(MOE_EXPERTS_PER_GROUP)]
    rowi = lax.broadcasted_iota(jnp.int32, (LANES, tm), 0)
    for g in range(MOE_GROUPS):
        slab = jnp.zeros((LANES, tm), F32)
        for e in range(MOE_EXPERTS_PER_GROUP):
            val = jnp.where(sel[g], wt[e], 0.0)
            slab = jnp.where(rowi == e, jnp.broadcast_to(val, (LANES, tm)), slab)
        cw_ref[g, 0] = slab.T


def _mix(o_list, lse_list, yb, proj, x, mod, wa, wb, wo, n2w, wr, br, tm=512):
    bsz, s, d = x.shape
    tok = lambda b, i: (b, i, 0)
    full = lambda b, i: (0, 0)
    gate_blk = GATE_COL0 // d
    in_specs = ([pl.BlockSpec((1, tm, A_GROUP_WIDTH), tok)] * 3
                + [pl.BlockSpec((1, tm, LANES), tok)] * 3
                + [pl.BlockSpec((1, tm, d), tok),
                   pl.BlockSpec((1, tm, d), lambda b, i: (b, i, gate_blk)),
                   pl.BlockSpec((1, tm, d), lambda b, i: (b, i, gate_blk + 1)),
                   pl.BlockSpec((1, tm, d), tok),
                   pl.BlockSpec((1, N_MOD, d), lambda b, i: (b, 0, 0)),
                   pl.BlockSpec(wa.shape, full),
                   pl.BlockSpec(wb.shape, full),
                   pl.BlockSpec(wo.shape, full),
                   pl.BlockSpec((1, d), full),
                   pl.BlockSpec(wr.shape, full),
                   pl.BlockSpec(br.shape, full)])
    return pl.pallas_call(
        _mix_kernel,
        grid=(bsz, s // tm),
        in_specs=in_specs,
        out_specs=[pl.BlockSpec((1, tm, d), tok),
                   pl.BlockSpec((1, tm, d), tok),
                   pl.BlockSpec((MOE_GROUPS, 1, tm, LANES), lambda b, i: (0, b, i, 0))],
        out_shape=[jax.ShapeDtypeStruct((bsz, s, d), F32),
                   jax.ShapeDtypeStruct((bsz, s, d), BF16),
                   jax.ShapeDtypeStruct((MOE_GROUPS, bsz, s, LANES), F32)],
        compiler_params=_cparams(("arbitrary", "arbitrary")),
        name="mix",
    )(*o_list, *lse_list, yb, proj, proj, x, mod, wa, wb, wo, n2w.reshape(1, d), wr, br)


def _moe_kernel(h2_ref, cw_ref, wg_ref, wu_ref, wd_ref, x1_ref, mod_ref, out_ref, acc_ref):
    g = pl.program_id(1)
    h = h2_ref[...]
    hid = _silu(jnp.dot(h, wg_ref[0], preferred_element_type=F32)) * jnp.dot(
        h, wu_ref[0], preferred_element_type=F32)
    cw = cw_ref[0]
    parts = [hid[:, e * MOE_D_FF:(e + 1) * MOE_D_FF] * cw[:, e:e + 1] for e in range(MOE_EXPERTS_PER_GROUP)]
    contrib = jnp.dot(jnp.concatenate(parts, axis=1).astype(BF16), wd_ref[0], preferred_element_type=F32)

    @pl.when(g == 0)
    def _():
        acc_ref[...] = contrib

    @pl.when(g > 0)
    def _():
        acc_ref[...] += contrib

    @pl.when(g == MOE_GROUPS - 1)
    def _():
        out_ref[...] = x1_ref[...] + mod_ref[0][5:6] * acc_ref[...]


def _moe(h2, cw, wg, wu, wd, x1, mod, seq, tm=1024):
    t, d = h2.shape
    ff = wg.shape[2]
    return pl.pallas_call(
        _moe_kernel,
        grid=(t // tm, MOE_GROUPS),
        in_specs=[pl.BlockSpec((tm, d), lambda i, g: (i, 0)),
                  pl.BlockSpec((1, tm, LANES), lambda i, g: (g, i, 0)),
                  pl.BlockSpec((1, d, ff), lambda i, g: (g, 0, 0)),
                  pl.BlockSpec((1, d, ff), lambda i, g: (g, 0, 0)),
                  pl.BlockSpec((1, ff, d), lambda i, g: (g, 0, 0)),
                  pl.BlockSpec((tm, d), lambda i, g: (i, 0)),
                  pl.BlockSpec((1, N_MOD, d), lambda i, g: ((i * tm) // seq, 0, 0))],
        out_specs=pl.BlockSpec((tm, d), lambda i, g: (i, 0)),
        out_shape=jax.ShapeDtypeStruct((t, d), F32),
        scratch_shapes=[pltpu.VMEM((tm, d), F32)],
        compiler_params=_cparams(("arbitrary", "arbitrary")),
        name="moe",
    )(h2, cw, wg, wu, wd, x1, mod)


def _prep_in_weights(w_in, conv_w, a_log, dt_bias, hb):
    d = w_in.shape[0]
    aw = 3 * A_GROUP_WIDTH
    bw = B_HEADS * HEAD_DIM
    offs = np.cumsum([0, aw, aw, aw, bw, bw, bw, bw, B_HEADS, B_HEADS, d, d])
    qa, ka, va, qb, kb, vb, zb, beta, a_raw, ga, gb = [w_in[:, offs[i]:offs[i + 1]] for i in range(11)]
    cols, conv_cols = [], []
    for h in range(B_HEADS):
        sl = slice(h * HEAD_DIM, (h + 1) * HEAD_DIM)
        cols += [qb[:, sl], kb[:, sl], vb[:, sl], zb[:, sl]]
        conv_cols += [conv_w[:, sl], conv_w[:, bw + h * HEAD_DIM:bw + (h + 1) * HEAD_DIM],
                      conv_w[:, 2 * bw + h * HEAD_DIM:2 * bw + (h + 1) * HEAD_DIM],
                      jnp.zeros((B_CONV, HEAD_DIM), conv_w.dtype)]
    cols += [ga, gb]
    for g in range(len(A_GROUPS)):
        sl = slice(g * A_GROUP_WIDTH, (g + 1) * A_GROUP_WIDTH)
        cols += [qa[:, sl], ka[:, sl], va[:, sl]]
    w1 = jnp.concatenate(cols, axis=1).astype(BF16)
    convw = jnp.concatenate(conv_cols, axis=1)
    ng = B_HEADS // hb
    pad = jnp.zeros((d, LANES - GDN_A_LANE - hb), w_in.dtype)
    gap = jnp.zeros((d, GDN_A_LANE - hb), w_in.dtype)
    ba_cols = []
    vec_pad = lambda v: jnp.concatenate([jnp.zeros((GDN_A_LANE,), F32), v, jnp.zeros((LANES - GDN_A_LANE - hb,), F32)])
    alog, dtb = [], []
    for gi in range(ng):
        sl = slice(gi * hb, (gi + 1) * hb)
        ba_cols += [beta[:, sl], gap, a_raw[:, sl], pad]
        alog.append(vec_pad(a_log[sl].astype(F32)))
        dtb.append(vec_pad(dt_bias[sl].astype(F32)))
    wba = jnp.concatenate(ba_cols, axis=1).astype(BF16)
    return w1, wba, convw, jnp.stack(alog)[:, None, :], jnp.stack(dtb)[:, None, :]


def _prep_router(w_rg, b_rg, w_re, b_re):
    d = w_rg.shape[0]
    n_exp = MOE_GROUPS * MOE_EXPERTS_PER_GROUP
    wr = jnp.concatenate([w_rg.T, jnp.zeros((ROUTER_EXPERT_ROW0 - MOE_GROUPS, d), F32), w_re.T,
                          jnp.zeros((ROUTER_ROWS - ROUTER_EXPERT_ROW0 - n_exp, d), F32)], axis=0)
    br = jnp.concatenate([b_rg, jnp.zeros((ROUTER_EXPERT_ROW0 - MOE_GROUPS,), F32), b_re.reshape(-1),
                          jnp.zeros((ROUTER_ROWS - ROUTER_EXPERT_ROW0 - n_exp,), F32)])
    return wr.astype(F32), br.astype(F32)[:, None]


def _prep_experts(w_gate, w_up, w_down):
    ne, d, ff = w_gate.shape
    per_group = lambda w: (w.reshape(MOE_GROUPS, MOE_EXPERTS_PER_GROUP, d, ff).transpose(0, 2, 1, 3)
                           .reshape(MOE_GROUPS, d, MOE_EXPERTS_PER_GROUP * ff).astype(BF16))
    wd = w_down.reshape(MOE_GROUPS, MOE_EXPERTS_PER_GROUP * ff, d).astype(BF16)
    return per_group(w_gate), per_group(w_up), wd


def kernel(x, c, w_ada, b_ada, norm1_w, w_in, conv_w, a_q_norm_w, a_k_norm_w, b_A_log, b_dt_bias, b_out_norm_w, w_branch_a, w_branch_b, w_o, norm2_w, w_router_group, b_router_group, w_router_expert, b_router_expert, w_exp_gate, w_exp_up, w_exp_down):
    bsz, s, d = x.shape
    hb = GDN_HEADS_PER_STEP
    for i in range(w_ada.shape[0]):
        mod = _ada(c, w_ada[i], b_ada[i]).reshape(bsz, N_MOD, d)
        w1, wba, convw, alog, dtb = _prep_in_weights(w_in[i], conv_w[i], b_A_log[i], b_dt_bias[i], hb)
        proj, ba = _inproj(x, mod, norm1_w[i], w1, wba)
        o_list, lse_list = [], []
        for g in range(len(A_GROUPS)):
            o_g, lse_g = _attention_group(proj, g, a_q_norm_w[i], a_k_norm_w[i])
            o_list.append(o_g)
            lse_list.append(lse_g)
        yb = _gdn(proj, ba, convw, alog, dtb, b_out_norm_w[i], hb)
        wr, br = _prep_router(w_router_group[i], b_router_group[i], w_router_expert[i], b_router_expert[i])
        x1, h2, cw = _mix(o_list, lse_list, yb, proj, x, mod, w_branch_a[i].astype(BF16),
                          w_branch_b[i].astype(BF16), w_o[i].astype(BF16), norm2_w[i], wr, br)
        wg, wu, wd = _prep_experts(w_exp_gate[i], w_exp_up[i], w_exp_down[i])
        out = _moe(h2.reshape(bsz * s, d), cw.reshape(MOE_GROUPS, bsz * s, LANES), wg, wu, wd,
                   x1.reshape(bsz * s, d), mod, s)
        x = out.reshape(bsz, s, d)
    return x
```

```python
import functools

import jax
import jax.numpy as jnp
import numpy as np
from jax import lax
from jax.experimental import pallas as pl
from jax.experimental.pallas import tpu as pltpu

F32 = jnp.float32
BF16 = jnp.bfloat16

NORM_EPS = 1e-6
N_MOD = 6

A_GROUPS = ((128, 1), (512, 4), (2048, 16))
A_HEADS_PER_GROUP = 4
HEAD_DIM = 128
A_GROUP_WIDTH = A_HEADS_PER_GROUP * HEAD_DIM
A_GROUP_COLS = 3 * A_GROUP_WIDTH
A_SPAN = 128

B_HEADS = 8
B_CONV = 4
B_CHUNK = 64
B_HEAD_COLS = 4 * HEAD_DIM
GDN_HEADS_PER_STEP = 4
GDN_A_LANE = 8

MOE_GROUPS = 4
MOE_EXPERTS_PER_GROUP = 4
MOE_D_FF = 256
ROUTER_ROWS = 32
ROUTER_EXPERT_ROW0 = 8
ROUTE_ROWS = 8
MOE_TILE = 512
MOE_BLOCK = 160
MOE_ROW_ALIGN = 16

GDN_COLS = B_HEADS * B_HEAD_COLS
GATE_COL0 = GDN_COLS
ATTN_COL0 = GDN_COLS + 2048
PROJ_COLS = ATTN_COL0 + 3 * A_GROUP_COLS

LANES = 128
VMEM_LIMIT_BYTES = 56 * 1024 * 1024

_NT = (((1,), (1,)), ((), ()))
_TN = (((0,), (0,)), ((), ()))


def _sigmoid(x):
    return 0.5 * jnp.tanh(0.5 * x) + 0.5


def _silu(x):
    return x * _sigmoid(x)


def _cparams(sem):
    return pltpu.CompilerParams(dimension_semantics=sem, vmem_limit_bytes=VMEM_LIMIT_BYTES)


def _ada_kernel(c_ref, w_ref, b_ref, o_ref):
    c = c_ref[...]
    o_ref[...] = jnp.dot(_silu(c), w_ref[...], preferred_element_type=F32,
                         precision=lax.Precision.HIGHEST) + b_ref[...]


def _ada(c, w_ada, b_ada):
    bsz, d = c.shape
    n = w_ada.shape[1]
    tn = 1536
    return pl.pallas_call(
        _ada_kernel,
        grid=(n // tn,),
        in_specs=[pl.BlockSpec((bsz, d), lambda j: (0, 0)),
                  pl.BlockSpec((d, tn), lambda j: (0, j)),
                  pl.BlockSpec((1, tn), lambda j: (0, j))],
        out_specs=pl.BlockSpec((bsz, tn), lambda j: (0, j)),
        out_shape=jax.ShapeDtypeStruct((bsz, n), F32),
        compiler_params=_cparams(("arbitrary",)),
        name="ada",
    )(c, w_ada, b_ada.reshape(1, n))


def _modulated_norm(x, norm_w, scale, shift):
    y = x * lax.rsqrt(jnp.mean(x * x, axis=-1, keepdims=True) + NORM_EPS)
    return (y * norm_w) * (1.0 + scale) + shift


def _inproj_kernel(x_ref, mod_ref, nw_ref, w_ref, wba_ref, proj_ref, ba_ref, h_ref):
    @pl.when(pl.program_id(2) == 0)
    def _():
        m = mod_ref[0]
        h = _modulated_norm(x_ref[0], nw_ref[...], m[1:2], m[0:1]).astype(BF16)
        h_ref[...] = h
        ba_ref[0] = jnp.dot(h, wba_ref[...], preferred_element_type=F32)

    proj_ref[0] = jnp.dot(h_ref[...], w_ref[...], preferred_element_type=F32).astype(BF16)


def _inproj(x, mod, norm_w, w1, wba, tm=1024, tn=1536):
    bsz, s, d = x.shape
    n = w1.shape[1]
    nba = wba.shape[1]
    return pl.pallas_call(
        _inproj_kernel,
        grid=(bsz, s // tm, n // tn),
        in_specs=[pl.BlockSpec((1, tm, d), lambda b, i, j: (b, i, 0)),
                  pl.BlockSpec((1, N_MOD, d), lambda b, i, j: (b, 0, 0)),
                  pl.BlockSpec((1, d), lambda b, i, j: (0, 0)),
                  pl.BlockSpec((d, tn), lambda b, i, j: (0, j)),
                  pl.BlockSpec((d, nba), lambda b, i, j: (0, 0))],
        out_specs=[pl.BlockSpec((1, tm, tn), lambda b, i, j: (b, i, j)),
                   pl.BlockSpec((1, tm, nba), lambda b, i, j: (b, i, 0))],
        out_shape=[jax.ShapeDtypeStruct((bsz, s, n), BF16),
                   jax.ShapeDtypeStruct((bsz, s, nba), F32)],
        scratch_shapes=[pltpu.VMEM((tm, d), BF16)],
        compiler_params=_cparams(("arbitrary", "arbitrary", "arbitrary")),
        name="inproj",
    )(x, mod, norm_w.reshape(1, d), w1, wba)


def _attn_kernel(qkv_ref, qw_ref, kw_ref, o_ref, lse_ref, qn_ref, kn_ref, vn_ref, *rm_refs, dil, length):
    seq = dil * length
    nb = length // A_SPAN
    if dil > 1:
        stage_ref, orm_ref, lrm_ref = rm_refs
    qw = qw_ref[...] * (HEAD_DIM ** -0.5)
    kw = kw_ref[...]

    def to_residue_major(x, dst_ref, row_off, c0):
        if dil == 1:
            dst_ref[row_off:row_off + seq, c0:c0 + HEAD_DIM] = x.astype(BF16)
        else:
            stage_ref[...] = x
            for r in range(dil):
                lo = row_off + r * length
                dst_ref[lo:lo + length, c0:c0 + HEAD_DIM] = (
                    stage_ref[pl.ds(r, length, stride=dil), :].astype(BF16))

    ones_sq = jnp.ones((HEAD_DIM, HEAD_DIM), BF16)

    def mean_sq(x):
        return jnp.dot((x * x).astype(BF16), ones_sq, preferred_element_type=F32) * (1.0 / HEAD_DIM)

    kn_ref[0:A_SPAN, :] = jnp.zeros((A_SPAN, A_GROUP_WIDTH), BF16)
    vn_ref[0:A_SPAN, :] = jnp.zeros((A_SPAN, A_GROUP_WIDTH), BF16)
    for hd in range(A_HEADS_PER_GROUP):
        c0 = hd * HEAD_DIM
        q = qkv_ref[0, :, c0:c0 + HEAD_DIM].astype(F32)
        k = qkv_ref[0, :, A_GROUP_WIDTH + c0:A_GROUP_WIDTH + c0 + HEAD_DIM].astype(F32)
        v = qkv_ref[0, :, 2 * A_GROUP_WIDTH + c0:2 * A_GROUP_WIDTH + c0 + HEAD_DIM].astype(F32)
        qn = q * lax.rsqrt(mean_sq(q) + NORM_EPS) * qw
        kn = k * lax.rsqrt(mean_sq(k) + NORM_EPS) * kw
        to_residue_major(qn, qn_ref, 0, c0)
        to_residue_major(kn, kn_ref, A_SPAN, c0)
        to_residue_major(v, vn_ref, A_SPAN, c0)

    qi = lax.broadcasted_iota(jnp.int32, (A_SPAN, 2 * A_SPAN), 0)
    kj = lax.broadcasted_iota(jnp.int32, (A_SPAN, 2 * A_SPAN), 1)
    dist = qi + A_SPAN - kj
    band = (dist >= 0) & (dist <= A_SPAN)
    lane = lax.broadcasted_iota(jnp.int32, (A_SPAN, LANES), 1)

    def body(bi, carry):
        row0 = pl.multiple_of(bi * A_SPAN, A_SPAN)
        first_key = jnp.where((bi & (nb - 1)) == 0, A_SPAN, 0)
        mask = band & (kj >= first_key)
        lse_tile = jnp.zeros((A_SPAN, LANES), F32)
        heads = range(A_HEADS_PER_GROUP)
        cols = [slice(hd * HEAD_DIM, (hd + 1) * HEAD_DIM) for hd in heads]
        s = [lax.dot_general(qn_ref[pl.ds(row0, A_SPAN), cols[hd]], kn_ref[pl.ds(row0, 2 * A_SPAN), cols[hd]],
                             _NT, preferred_element_type=F32) for hd in heads]
        s = [jnp.where(mask, s[hd], -jnp.inf) for hd in heads]
        m = [jnp.max(s[hd], axis=-1, keepdims=True) for hd in heads]
        p = [jnp.exp(s[hd] - m[hd]) for hd in heads]
        den = [jnp.sum(p[hd], axis=-1, keepdims=True) for hd in heads]
        pv = [jnp.dot(p[hd].astype(BF16), vn_ref[pl.ds(row0, 2 * A_SPAN), cols[hd]],
                      preferred_element_type=F32) for hd in heads]
        for hd in heads:
            o = pv[hd] / den[hd]
            if dil == 1:
                o_ref[0, pl.ds(row0, A_SPAN), cols[hd]] = o.astype(BF16)
            else:
                orm_ref[pl.ds(row0, A_SPAN), cols[hd]] = o
            lse_tile = jnp.where(lane == hd, m[hd] + jnp.log(den[hd]), lse_tile)
        if dil == 1:
            lse_ref[0, pl.ds(row0, A_SPAN), :] = lse_tile
        else:
            lrm_ref[pl.ds(row0, A_SPAN), :] = lse_tile
        return carry

    lax.fori_loop(0, seq // A_SPAN, body, 0)

    if dil > 1:
        for hd in range(A_HEADS_PER_GROUP):
            c0 = hd * HEAD_DIM
            for r in range(dil):
                stage_ref[pl.ds(r, length, stride=dil), :] = orm_ref[r * length:(r + 1) * length, c0:c0 + HEAD_DIM]
            o_ref[0, :, c0:c0 + HEAD_DIM] = stage_ref[...].astype(BF16)
        for r in range(dil):
            stage_ref[pl.ds(r, length, stride=dil), :] = lrm_ref[r * length:(r + 1) * length, :]
        lse_ref[0] = stage_ref[...]


def _attention_group(proj, g, qw, kw):
    bsz, s, _ = proj.shape
    _, dil = A_GROUPS[g]
    length = s // dil
    col_blk = ATTN_COL0 // A_GROUP_COLS + g
    scratch = [pltpu.VMEM((s, A_GROUP_WIDTH), BF16),
               pltpu.VMEM((s + A_SPAN, A_GROUP_WIDTH), BF16),
               pltpu.VMEM((s + A_SPAN, A_GROUP_WIDTH), BF16)]
    if dil > 1:
        scratch += [pltpu.VMEM((s, HEAD_DIM), F32),
                    pltpu.VMEM((s, A_GROUP_WIDTH), F32),
                    pltpu.VMEM((s, LANES), F32)]
    return pl.pallas_call(
        functools.partial(_attn_kernel, dil=dil, length=length),
        grid=(bsz,),
        in_specs=[pl.BlockSpec((1, s, A_GROUP_COLS), lambda b: (b, 0, col_blk)),
                  pl.BlockSpec((1, HEAD_DIM), lambda b: (0, 0)),
                  pl.BlockSpec((1, HEAD_DIM), lambda b: (0, 0))],
        out_specs=[pl.BlockSpec((1, s, A_GROUP_WIDTH), lambda b: (b, 0, 0)),
                   pl.BlockSpec((1, s, LANES), lambda b: (b, 0, 0))],
        out_shape=[jax.ShapeDtypeStruct((bsz, s, A_GROUP_WIDTH), BF16),
                   jax.ShapeDtypeStruct((bsz, s, LANES), F32)],
        scratch_shapes=scratch,
        compiler_params=_cparams(("arbitrary",)),
        name=f"attn{g}",
    )(proj, qw.reshape(1, HEAD_DIM), kw.reshape(1, HEAD_DIM))


def _split3_bf16(x):
    hi = x.astype(BF16)
    r1 = x - hi.astype(F32)
    mid = r1.astype(BF16)
    lo = (r1 - mid.astype(F32)).astype(BF16)
    return hi, mid, lo


def _bdot(a, b):
    return jnp.einsum("nik,nkj->nij", a, b, preferred_element_type=F32)


def _gdn_kernel(proj_ref, ba_ref, convw_ref, alog_ref, dtb_ref, onw_ref, y_ref,
                u_ref, w_ref, qd_ref, ket_ref, intra_ref, egl_ref, st_ref, *, hb, seq):
    c = B_CHUNK
    nc = seq // c
    row8 = lax.broadcasted_iota(jnp.int32, (8, LANES), 0)

    ba = ba_ref[0]
    beta_all = _sigmoid(ba)
    xs = ba + dtb_ref[0]
    softplus = jnp.maximum(xs, 0.0) + jnp.log(1.0 + jnp.exp(-jnp.abs(xs)))
    g_all = -jnp.exp(alog_ref[0]) * softplus

    ii = lax.broadcasted_iota(jnp.int32, (c, c), 0)
    jj = lax.broadcasted_iota(jnp.int32, (c, c), 1)
    causal = (ii >= jj)[None]
    strict = (ii > jj)[None]
    strict_f = (ii > jj).astype(F32)[None]
    tril3 = jnp.broadcast_to(
        jnp.concatenate([(ii >= jj).astype(BF16)] * 3, axis=1)[None], (nc, c, 3 * c))
    ones_sq = jnp.ones((HEAD_DIM, HEAD_DIM), BF16)

    gc3 = _bdot(tril3, jnp.concatenate(_split3_bf16(g_all.reshape(nc, c, LANES)), axis=1))
    gc = gc3.reshape(seq, LANES)
    glast = jnp.broadcast_to(gc3[:, c - 1:c, :], (nc, c, LANES)).reshape(seq, LANES)
    e_gc = jnp.exp(gc)
    e_rem = jnp.exp(glast - gc)
    e_gl = jnp.exp(glast)

    def row_sumsq(x):
        return jnp.dot((x * x).astype(BF16), ones_sq, preferred_element_type=F32)

    def conv_silu(col0):
        x = proj_ref[0, :, col0:col0 + HEAD_DIM].astype(F32)
        cw = convw_ref[:, col0:col0 + HEAD_DIM]
        acc = x * cw[B_CONV - 1:B_CONV]
        for sh in range(1, B_CONV):
            rolled = pltpu.roll(x, sh, axis=0)
            head = jnp.where(row8 >= sh, rolled[0:8], 0.0)
            xs_ = jnp.concatenate([head, rolled[8:]], axis=0)
            acc = acc + xs_ * cw[B_CONV - 1 - sh:B_CONV - sh]
        return _silu(acc)

    for j in range(hb):
        c0 = j * B_HEAD_COLS
        q = conv_silu(c0)
        k = conv_silu(c0 + HEAD_DIM)
        v = conv_silu(c0 + 2 * HEAD_DIM)
        q = q * lax.rsqrt(row_sumsq(q) + NORM_EPS) * (HEAD_DIM ** -0.5)
        k = k * lax.rsqrt(row_sumsq(k) + NORM_EPS)
        a_lane = GDN_A_LANE + j
        beta = beta_all[:, j:j + 1]
        g_col = g_all[:, a_lane:a_lane + 1]
        egc = e_gc[:, a_lane:a_lane + 1]
        kb = k * beta
        k3 = k.astype(BF16).reshape(nc, c, HEAD_DIM)
        kb3 = kb.astype(BF16).reshape(nc, c, HEAD_DIM)
        q3 = q.astype(BF16).reshape(nc, c, HEAD_DIM)
        kk = jnp.einsum("nik,njk->nij", kb3, k3, preferred_element_type=F32)
        qk = jnp.einsum("nik,njk->nij", q3, k3, preferred_element_type=F32)
        bm = g_col.reshape(nc, c, 1) * strict_f
        diff = _bdot(tril3, jnp.concatenate(_split3_bf16(bm), axis=1))
        decay = jnp.exp(diff)
        lower = jnp.where(strict, kk * decay, 0.0)
        intra = jnp.where(causal, qk * decay, 0.0)
        m_pow = -lower
        n_acc = m_pow
        for _ in range(5):
            mb = m_pow.astype(BF16)
            m_pow = _bdot(mb, mb)
            n_acc = n_acc + m_pow + _bdot(n_acc.astype(BF16), m_pow.astype(BF16))
        rhs = jnp.concatenate([v * beta, kb * egc], axis=-1).reshape(nc, c, 2 * HEAD_DIM)
        sol = rhs + _bdot(n_acc.astype(BF16), rhs.astype(BF16))
        u_ref[j] = sol[:, :, :HEAD_DIM].reshape(seq, HEAD_DIM)
        w_ref[j] = sol[:, :, HEAD_DIM:].reshape(seq, HEAD_DIM).astype(BF16)
        qd_ref[j] = (q * egc).astype(BF16)
        ke3 = (k * e_rem[:, a_lane:a_lane + 1]).reshape(nc, c, HEAD_DIM)
        ket_ref[j] = jnp.swapaxes(ke3, 1, 2).reshape(nc * HEAD_DIM, c).astype(BF16)
        intra_ref[j] = intra.reshape(seq, c).astype(BF16)
        egl_col = e_gl[:, a_lane:a_lane + 1].reshape(nc, c, 1)[:, 0, :]
        egl_ref[j] = jnp.broadcast_to(egl_col, (nc, LANES))

    st_ref[...] = jnp.zeros_like(st_ref)
    onw = onw_ref[...]

    def body(n, carry):
        r0 = pl.multiple_of(n * c, c)
        k0 = pl.multiple_of(n * HEAD_DIM, HEAD_DIM)
        heads = range(hb)
        state = [st_ref[j] for j in heads]
        ws = [jnp.dot(jnp.concatenate([w_ref[j, pl.ds(r0, c), :], qd_ref[j, pl.ds(r0, c), :]], axis=0),
                      state[j].astype(BF16), preferred_element_type=F32) for j in heads]
        v_new = [(u_ref[j, pl.ds(r0, c), :] - ws[j][:c]).astype(BF16) for j in heads]
        kv = [jnp.dot(ket_ref[j, pl.ds(k0, HEAD_DIM), :], v_new[j], preferred_element_type=F32) for j in heads]
        o = [ws[j][c:] + jnp.dot(intra_ref[j, pl.ds(r0, c), :], v_new[j], preferred_element_type=F32)
             for j in heads]
        for j in heads:
            st_ref[j] = state[j] * egl_ref[j, pl.ds(n, 1), :] + kv[j]
        ss = [row_sumsq(o[j]) for j in heads]
        for j in heads:
            zc = j * B_HEAD_COLS + 3 * HEAD_DIM
            z = proj_ref[0, pl.ds(r0, c), zc:zc + HEAD_DIM].astype(F32)
            on = o[j] * lax.rsqrt(ss[j] * (1.0 / HEAD_DIM) + NORM_EPS) * onw
            y_ref[0, pl.ds(r0, c), j * HEAD_DIM:(j + 1) * HEAD_DIM] = (on * _silu(z)).astype(BF16)
        return carry

    lax.fori_loop(0, nc, body, 0, unroll=2)


def _gdn(proj, ba, convw, alog, dtb, onw, hb=GDN_HEADS_PER_STEP):
    bsz, s, _ = proj.shape
    ng = B_HEADS // hb
    nc = s // B_CHUNK
    return pl.pallas_call(
        functools.partial(_gdn_kernel, hb=hb, seq=s),
        grid=(bsz, ng),
        in_specs=[pl.BlockSpec((1, s, hb * B_HEAD_COLS), lambda b, g: (b, 0, g)),
                  pl.BlockSpec((1, s, LANES), lambda b, g: (b, 0, g)),
                  pl.BlockSpec((B_CONV, hb * B_HEAD_COLS), lambda b, g: (0, g)),
                  pl.BlockSpec((1, 1, LANES), lambda b, g: (g, 0, 0)),
                  pl.BlockSpec((1, 1, LANES), lambda b, g: (g, 0, 0)),
                  pl.BlockSpec((1, HEAD_DIM), lambda b, g: (0, 0))],
        out_specs=pl.BlockSpec((1, s, hb * HEAD_DIM), lambda b, g: (b, 0, g)),
        out_shape=jax.ShapeDtypeStruct((bsz, s, B_HEADS * HEAD_DIM), BF16),
        scratch_shapes=[pltpu.VMEM((hb, s, HEAD_DIM), F32),
                        pltpu.VMEM((hb, s, HEAD_DIM), BF16),
                        pltpu.VMEM((hb, s, HEAD_DIM), BF16),
                        pltpu.VMEM((hb, nc * HEAD_DIM, B_CHUNK), BF16),
                        pltpu.VMEM((hb, s, B_CHUNK), BF16),
                        pltpu.VMEM((hb, nc, LANES), F32),
                        pltpu.VMEM((hb, HEAD_DIM, HEAD_DIM), F32)],
        compiler_params=_cparams(("arbitrary", "arbitrary")),
        name="gdn",
    )(proj, ba, convw, alog, dtb, onw.reshape(1, HEAD_DIM))


def _mix_kernel(o0_ref, o1_ref, o2_ref, l0_ref, l1_ref, l2_ref, yb_ref, ga_ref, gb_ref, x_ref,
                mod_ref, wa_ref, wb_ref, wo_ref, n2_ref, wr_ref, br_ref,
                x1_ref, h2_ref, route_ref):
    tm = x_ref.shape[1]
    o_refs = (o0_ref, o1_ref, o2_ref)
    ls = [l0_ref[0], l1_ref[0], l2_ref[0]]
    mx = jnp.maximum(jnp.maximum(ls[0], ls[1]), ls[2])
    es = [jnp.exp(l - mx) for l in ls]
    den = es[0] + es[1] + es[2]
    ws = [e / den for e in es]
    heads = []
    for hd in range(A_HEADS_PER_GROUP):
        c0 = hd * HEAD_DIM
        acc = ws[0][:, hd:hd + 1] * o_refs[0][0, :, c0:c0 + HEAD_DIM].astype(F32)
        for g in (1, 2):
            acc = acc + ws[g][:, hd:hd + 1] * o_refs[g][0, :, c0:c0 + HEAD_DIM].astype(F32)
        heads.append(acc)
    ya = jnp.concatenate(heads, axis=1).astype(BF16)
    a = jnp.dot(ya, wa_ref[...], preferred_element_type=F32)
    bm = jnp.dot(yb_ref[0], wb_ref[...], preferred_element_type=F32)
    merged = _sigmoid(ga_ref[0].astype(F32)) * a + _sigmoid(gb_ref[0].astype(F32)) * bm
    mix = jnp.dot(merged.astype(BF16), wo_ref[...], preferred_element_type=F32)
    m = mod_ref[0]
    x1 = x_ref[0] + m[2:3] * mix
    x1_ref[0] = x1
    h2 = _modulated_norm(x1, n2_ref[...], m[4:5], m[3:4])
    h2_ref[0] = h2.astype(BF16)

    lt = lax.dot_general(wr_ref[...], h2, _NT, preferred_element_type=F32,
                         precision=lax.Precision.HIGHEST) + br_ref[...]
    gl = [lt[i:i + 1] for i in range(MOE_GROUPS)]
    gm = jnp.maximum(jnp.maximum(gl[0], gl[1]), jnp.maximum(gl[2], gl[3]))
    gsum = sum(jnp.exp(x - gm) for x in gl)
    p_top = 1.0 / gsum
    sel, taken = [], jnp.zeros(gm.shape, jnp.bool_)
    for i in range(MOE_GROUPS):
        s_i = jnp.logical_and(gl[i] == gm, jnp.logical_not(taken))
        taken = jnp.logical_or(taken, s_i)
        sel.append(s_i)
    ig = []
    for e in range(MOE_EXPERTS_PER_GROUP):
        acc = jnp.zeros(gm.shape, F32)
        for g in range(MOE_GROUPS):
            r = ROUTER_EXPERT_ROW0 + g * MOE_EXPERTS_PER_GROUP + e
            acc = jnp.where(sel[g], lt[r:r + 1], acc)
        ig.append(acc)
    t1 = jnp.maximum(jnp.maximum(ig[0], ig[1]), jnp.maximum(ig[2], ig[3]))
    is1, taken = [], jnp.zeros(gm.shape, jnp.bool_)
    for e in range(MOE_EXPERTS_PER_GROUP):
        s_e = jnp.logical_and(ig[e] == t1, jnp.logical_not(taken))
        taken = jnp.logical_or(taken, s_e)
        is1.append(s_e)
    rest = [jnp.where(is1[e], -jnp.inf, ig[e]) for e in range(MOE_EXPERTS_PER_GROUP)]
    t2 = jnp.maximum(jnp.maximum(rest[0], rest[1]), jnp.maximum(rest[2], rest[3]))
    is2, taken = [], jnp.zeros(gm.shape, jnp.bool_)
    for e in range(MOE_EXPERTS_PER_GROUP):
        s_e = jnp.logical_and(rest[e] == t2, jnp.logical_not(taken))
        taken = jnp.logical_or(taken, s_e)
        is2.append(s_e)
    e2 = jnp.exp(t2 - t1)
    w1 = p_top / (1.0 + e2)
    w2 = p_top * e2 / (1.0 + e2)
    wt = [jnp.where(is1[e], w1, 0.0) + jnp.where(is2[e], w2, 0.0) for e in range(MOE_EXPERTS_PER_GROUP)]
    gid = jnp.zeros(gm.shape, F32)
    for g in range(1, MOE_GROUPS):
        gid = jnp.where(sel[g], float(g), gid)
    rowi = lax.broadcasted_iota(jnp.int32, (ROUTE_ROWS, tm), 0)
    slab = jnp.where(rowi == 0, jnp.broadcast_to(gid, (ROUTE_ROWS, tm)), 0.0)
    for e in range(MOE_EXPERTS_PER_GROUP):
        slab = jnp.where(rowi == 1 + e, jnp.broadcast_to(wt[e], (ROUTE_ROWS, tm)), slab)
    route_ref[0] = slab


def _mix(o_list, lse_list, yb, proj, x, mod, wa, wb, wo, n2w, wr, br, tm=512):
    bsz, s, d = x.shape
    tok = lambda b, i: (b, i, 0)
    full = lambda b, i: (0, 0)
    gate_blk = GATE_COL0 // d
    in_specs = ([pl.BlockSpec((1, tm, A_GROUP_WIDTH), tok)] * 3
                + [pl.BlockSpec((1, tm, LANES), tok)] * 3
                + [pl.BlockSpec((1, tm, d), tok),
                   pl.BlockSpec((1, tm, d), lambda b, i: (b, i, gate_blk)),
                   pl.BlockSpec((1, tm, d), lambda b, i: (b, i, gate_blk + 1)),
                   pl.BlockSpec((1, tm, d), tok),
                   pl.BlockSpec((1, N_MOD, d), lambda b, i: (b, 0, 0)),
                   pl.BlockSpec(wa.shape, full),
                   pl.BlockSpec(wb.shape, full),
                   pl.BlockSpec(wo.shape, full),
                   pl.BlockSpec((1, d), full),
                   pl.BlockSpec(wr.shape, full),
                   pl.BlockSpec(br.shape, full)])
    return pl.pallas_call(
        _mix_kernel,
        grid=(bsz, s // tm),
        in_specs=in_specs,
        out_specs=[pl.BlockSpec((1, tm, d), tok),
                   pl.BlockSpec((1, tm, d), tok),
                   pl.BlockSpec((1, ROUTE_ROWS, tm), lambda b, i: (b, 0, i))],
        out_shape=[jax.ShapeDtypeStruct((bsz, s, d), F32),
                   jax.ShapeDtypeStruct((bsz, s, d), BF16),
                   jax.ShapeDtypeStruct((bsz, ROUTE_ROWS, s), F32)],
        compiler_params=_cparams(("arbitrary", "arbitrary")),
        name="mix",
    )(*o_list, *lse_list, yb, proj, proj, x, mod, wa, wb, wo, n2w.reshape(1, d), wr, br)


def _moe_kernel(h2_ref, route_ref, wg_ref, wu_ref, wd_ref, x1_ref, mod_ref, out_ref, xa_ref, ys_ref):
    tm, blk = MOE_TILE, MOE_BLOCK
    d = h2_ref.shape[1]
    n_slots = ys_ref.shape[0]
    r = route_ref[0]
    rowi = lax.broadcasted_iota(jnp.int32, (ROUTE_ROWS, tm), 0)
    gid = jnp.broadcast_to(r[0:1], (ROUTE_ROWS, tm))
    onehot = jnp.where((gid == rowi.astype(F32)) & (rowi < MOE_GROUPS), 1.0, 0.0)
    oh_b = onehot.astype(BF16)
    kk = lax.broadcasted_iota(jnp.int32, (tm, tm), 0)
    tt = lax.broadcasted_iota(jnp.int32, (tm, tm), 1)
    before = jnp.dot(oh_b, (kk < tt).astype(BF16), preferred_element_type=F32)
    count = jnp.dot(oh_b, jnp.ones((tm, tm), BF16), preferred_element_type=F32)
    nblk = jnp.floor((count + (blk - 1)) * (1.0 / blk))
    slot = jnp.zeros((1, tm), F32)
    base = jnp.zeros((1, tm), F32)
    for g in range(MOE_GROUPS):
        slot = slot + onehot[g:g + 1] * (base + before[g:g + 1])
        base = base + nblk[g:g + 1] * blk
    slot_i = slot.astype(jnp.int32)

    slab = jnp.where(rowi == 0, jnp.broadcast_to(slot, (ROUTE_ROWS, tm)), r)
    slab_t = jnp.concatenate([slab, jnp.zeros((LANES - ROUTE_ROWS, tm), F32)], axis=0).T
    slot_col = jnp.broadcast_to(slab_t[:, 0:1], (tm, LANES)).astype(jnp.int32)
    cw_hi = slab_t.astype(BF16)
    cw_lo = (slab_t - cw_hi.astype(F32)).astype(BF16)
    xa_ref[:, 0:d] = h2_ref[...]
    xa_ref[:, d:d + LANES] = cw_hi
    xa_ref[:, d + LANES:d + 2 * LANES] = cw_lo

    done = jnp.int32(0)
    for g in range(MOE_GROUPS):
        n_g = (jnp.sum(onehot[g:g + 1]).astype(jnp.int32) + (blk - 1)) // blk

        def body(j, carry, g=g, first=done):
            s0 = pl.multiple_of((first + j) * blk, MOE_ROW_ALIGN)
            rows = lax.broadcasted_iota(jnp.int32, (blk, tm), 0) + s0
            gather = (rows == jnp.broadcast_to(slot_i, (blk, tm))).astype(BF16)
            xb = jnp.dot(gather, xa_ref[...], preferred_element_type=F32)
            hb = xb[:, 0:d].astype(BF16)
            cwb = xb[:, d:d + LANES] + xb[:, d + LANES:d + 2 * LANES]
            experts = range(g * MOE_EXPERTS_PER_GROUP, (g + 1) * MOE_EXPERTS_PER_GROUP)
            gate = [jnp.dot(hb, wg_ref[e], preferred_element_type=F32) for e in experts]
            up = [jnp.dot(hb, wu_ref[e], preferred_element_type=F32) for e in experts]
            hid = [(_silu(gate[i]) * up[i] * cwb[:, 1 + i:2 + i]).astype(BF16)
                   for i in range(MOE_EXPERTS_PER_GROUP)]
            y = jnp.dot(hid[0], wd_ref[experts[0]], preferred_element_type=F32)
            for i in range(1, MOE_EXPERTS_PER_GROUP):
                y = y + jnp.dot(hid[i], wd_ref[experts[i]], preferred_element_type=F32)
            ys_ref[pl.ds(s0, blk), :] = y.astype(BF16)
            return carry

        lax.fori_loop(0, n_g, body, 0)
        done = done + n_g

    def zero_block(j, carry):
        ys_ref[pl.ds(pl.multiple_of(j * blk, MOE_ROW_ALIGN), blk), :] = jnp.zeros((blk, d), BF16)
        return carry

    max_blocks = tm // blk + MOE_GROUPS
    lax.fori_loop(done, max_blocks, zero_block, 0)
    if n_slots > max_blocks * blk:
        ys_ref[max_blocks * blk:n_slots, :] = jnp.zeros((n_slots - max_blocks * blk, d), BF16)
    cols = lax.broadcasted_iota(jnp.int32, (tm, LANES), 1)
    scatter = jnp.concatenate([(cols + c0 == slot_col).astype(BF16) for c0 in range(0, n_slots, LANES)], axis=1)
    moe = jnp.dot(scatter, ys_ref[...], preferred_element_type=F32)
    out_ref[...] = x1_ref[...] + mod_ref[0][5:6] * moe


def _moe(h2, route, wg, wu, wd, x1, mod, seq):
    t, d = h2.shape
    tm = MOE_TILE
    n_slots = -(-((tm // MOE_BLOCK + MOE_GROUPS) * MOE_BLOCK) // LANES) * LANES
    tiles_per_seq = seq // tm
    resident = lambda shape: pl.BlockSpec(shape, lambda i: (0, 0, 0), pipeline_mode=pl.Buffered(1))
    return pl.pallas_call(
        _moe_kernel,
        grid=(t // tm,),
        in_specs=[pl.BlockSpec((tm, d), lambda i: (i, 0)),
                  pl.BlockSpec((1, ROUTE_ROWS, tm), lambda i: (i // tiles_per_seq, 0, i % tiles_per_seq)),
                  resident(wg.shape), resident(wu.shape), resident(wd.shape),
                  pl.BlockSpec((tm, d), lambda i: (i, 0)),
                  pl.BlockSpec((1, N_MOD, d), lambda i: (i // tiles_per_seq, 0, 0))],
        out_specs=pl.BlockSpec((tm, d), lambda i: (i, 0)),
        out_shape=jax.ShapeDtypeStruct((t, d), F32),
        scratch_shapes=[pltpu.VMEM((tm, d + 2 * LANES), BF16),
                        pltpu.VMEM((n_slots, d), BF16)],
        compiler_params=_cparams(("arbitrary",)),
        name="moe",
    )(h2, route, wg, wu, wd, x1, mod)


def _prep_in_weights(w_in, conv_w, a_log, dt_bias, hb):
    d = w_in.shape[0]
    aw = 3 * A_GROUP_WIDTH
    bw = B_HEADS * HEAD_DIM
    offs = np.cumsum([0, aw, aw, aw, bw, bw, bw, bw, B_HEADS, B_HEADS, d, d])
    qa, ka, va, qb, kb, vb, zb, beta, a_raw, ga, gb = [w_in[:, offs[i]:offs[i + 1]] for i in range(11)]
    cols, conv_cols = [], []
    for h in range(B_HEADS):
        sl = slice(h * HEAD_DIM, (h + 1) * HEAD_DIM)
        cols += [qb[:, sl], kb[:, sl], vb[:, sl], zb[:, sl]]
        conv_cols += [conv_w[:, sl], conv_w[:, bw + h * HEAD_DIM:bw + (h + 1) * HEAD_DIM],
                      conv_w[:, 2 * bw + h * HEAD_DIM:2 * bw + (h + 1) * HEAD_DIM],
                      jnp.zeros((B_CONV, HEAD_DIM), conv_w.dtype)]
    cols += [ga, gb]
    for g in range(len(A_GROUPS)):
        sl = slice(g * A_GROUP_WIDTH, (g + 1) * A_GROUP_WIDTH)
        cols += [qa[:, sl], ka[:, sl], va[:, sl]]
    w1 = jnp.concatenate(cols, axis=1).astype(BF16)
    convw = jnp.concatenate(conv_cols, axis=1)
    ng = B_HEADS // hb
    pad = jnp.zeros((d, LANES - GDN_A_LANE - hb), w_in.dtype)
    gap = jnp.zeros((d, GDN_A_LANE - hb), w_in.dtype)
    ba_cols = []
    vec_pad = lambda v: jnp.concatenate([jnp.zeros((GDN_A_LANE,), F32), v, jnp.zeros((LANES - GDN_A_LANE - hb,), F32)])
    alog, dtb = [], []
    for gi in range(ng):
        sl = slice(gi * hb, (gi + 1) * hb)
        ba_cols += [beta[:, sl], gap, a_raw[:, sl], pad]
        alog.append(vec_pad(a_log[sl].astype(F32)))
        dtb.append(vec_pad(dt_bias[sl].astype(F32)))
    wba = jnp.concatenate(ba_cols, axis=1).astype(BF16)
    return w1, wba, convw, jnp.stack(alog)[:, None, :], jnp.stack(dtb)[:, None, :]


def _prep_router(w_rg, b_rg, w_re, b_re):
    d = w_rg.shape[0]
    n_exp = MOE_GROUPS * MOE_EXPERTS_PER_GROUP
    wr = jnp.concatenate([w_rg.T, jnp.zeros((ROUTER_EXPERT_ROW0 - MOE_GROUPS, d), F32), w_re.T,
                          jnp.zeros((ROUTER_ROWS - ROUTER_EXPERT_ROW0 - n_exp, d), F32)], axis=0)
    br = jnp.concatenate([b_rg, jnp.zeros((ROUTER_EXPERT_ROW0 - MOE_GROUPS,), F32), b_re.reshape(-1),
                          jnp.zeros((ROUTER_ROWS - ROUTER_EXPERT_ROW0 - n_exp,), F32)])
    return wr.astype(F32), br.astype(F32)[:, None]


def kernel(x, c, w_ada, b_ada, norm1_w, w_in, conv_w, a_q_norm_w, a_k_norm_w, b_A_log, b_dt_bias, b_out_norm_w, w_branch_a, w_branch_b, w_o, norm2_w, w_router_group, b_router_group, w_router_expert, b_router_expert, w_exp_gate, w_exp_up, w_exp_down):
    bsz, s, d = x.shape
    hb = GDN_HEADS_PER_STEP
    for i in range(w_ada.shape[0]):
        mod = _ada(c, w_ada[i], b_ada[i]).reshape(bsz, N_MOD, d)
        w1, wba, convw, alog, dtb = _prep_in_weights(w_in[i], conv_w[i], b_A_log[i], b_dt_bias[i], hb)
        proj, ba = _inproj(x, mod, norm1_w[i], w1, wba)
        o_list, lse_list = [], []
        for g in range(len(A_GROUPS)):
            o_g, lse_g = _attention_group(proj, g, a_q_norm_w[i], a_k_norm_w[i])
            o_list.append(o_g)
            lse_list.append(lse_g)
        yb = _gdn(proj, ba, convw, alog, dtb, b_out_norm_w[i], hb)
        wr, br = _prep_router(w_router_group[i], b_router_group[i], w_router_expert[i], b_router_expert[i])
        x1, h2, route = _mix(o_list, lse_list, yb, proj, x, mod, w_branch_a[i].astype(BF16),
                          w_branch_b[i].astype(BF16), w_o[i].astype(BF16), norm2_w[i], wr, br)
        out = _moe(h2.reshape(bsz * s, d), route, w_exp_gate[i].astype(BF16), w_exp_up[i].astype(BF16),
                   w_exp_down[i].astype(BF16), x1.reshape(bsz * s, d), mod, s)
        x = out.reshape(bsz, s, d)
    return x
```

```python
import functools

import jax
import jax.numpy as jnp
import numpy as np
from jax import lax
from jax.experimental import pallas as pl
from jax.experimental.pallas import tpu as pltpu

F32 = jnp.float32
BF16 = jnp.bfloat16

NORM_EPS = 1e-6
N_MOD = 6

A_GROUPS = ((128, 1), (512, 4), (2048, 16))
A_HEADS_PER_GROUP = 4
HEAD_DIM = 128
A_GROUP_WIDTH = A_HEADS_PER_GROUP * HEAD_DIM
A_GROUP_COLS = 3 * A_GROUP_WIDTH
A_SPAN = 128
A_BLOCKS_PER_ITER = 4

B_HEADS = 8
B_CONV = 4
B_CHUNK = 64
B_HEAD_COLS = 4 * HEAD_DIM
GDN_HEADS_PER_STEP = 4
GDN_A_LANE = 8

MOE_GROUPS = 4
MOE_EXPERTS_PER_GROUP = 4
MOE_D_FF = 256
ROUTER_ROWS = 32
ROUTER_EXPERT_ROW0 = 8
ROUTE_ROWS = 8
MOE_TILE = 512
MOE_BLOCK = 160
MOE_ROW_ALIGN = 16

GDN_COLS = B_HEADS * B_HEAD_COLS
GATE_COL0 = GDN_COLS
ATTN_COL0 = GDN_COLS + 2048
PROJ_COLS = ATTN_COL0 + 3 * A_GROUP_COLS

LANES = 128
VMEM_LIMIT_BYTES = 56 * 1024 * 1024

_NT = (((1,), (1,)), ((), ()))
_TN = (((0,), (0,)), ((), ()))


def _sigmoid(x):
    return 0.5 * jnp.tanh(0.5 * x) + 0.5


def _silu(x):
    return x * _sigmoid(x)


def _cparams(sem):
    return pltpu.CompilerParams(dimension_semantics=sem, vmem_limit_bytes=VMEM_LIMIT_BYTES)


def _ada_kernel(c_ref, w_ref, b_ref, o_ref):
    c = c_ref[...]
    o_ref[...] = jnp.dot(_silu(c), w_ref[...], preferred_element_type=F32,
                         precision=lax.Precision.HIGHEST) + b_ref[...]


def _ada(c, w_ada, b_ada):
    bsz, d = c.shape
    n = w_ada.shape[1]
    tn = 1536
    return pl.pallas_call(
        _ada_kernel,
        grid=(n // tn,),
        in_specs=[pl.BlockSpec((bsz, d), lambda j: (0, 0)),
                  pl.BlockSpec((d, tn), lambda j: (0, j)),
                  pl.BlockSpec((1, tn), lambda j: (0, j))],
        out_specs=pl.BlockSpec((bsz, tn), lambda j: (0, j)),
        out_shape=jax.ShapeDtypeStruct((bsz, n), F32),
        compiler_params=_cparams(("arbitrary",)),
        name="ada",
    )(c, w_ada, b_ada.reshape(1, n))


def _modulated_norm(x, norm_w, scale, shift):
    y = x * lax.rsqrt(jnp.mean(x * x, axis=-1, keepdims=True) + NORM_EPS)
    return (y * norm_w) * (1.0 + scale) + shift


def _inproj_kernel(x_ref, mod_ref, nw_ref, w_ref, wba_ref, proj_ref, ba_ref, h_ref):
    @pl.when(pl.program_id(2) == 0)
    def _():
        m = mod_ref[0]
        h = _modulated_norm(x_ref[0], nw_ref[...], m[1:2], m[0:1]).astype(BF16)
        h_ref[...] = h
        ba_ref[0] = jnp.dot(h, wba_ref[...], preferred_element_type=F32)

    proj_ref[0] = jnp.dot(h_ref[...], w_ref[...], preferred_element_type=F32).astype(BF16)


def _inproj(x, mod, norm_w, w1, wba, tm=1024, tn=1536):
    bsz, s, d = x.shape
    n = w1.shape[1]
    nba = wba.shape[1]
    return pl.pallas_call(
        _inproj_kernel,
        grid=(bsz, s // tm, n // tn),
        in_specs=[pl.BlockSpec((1, tm, d), lambda b, i, j: (b, i, 0)),
                  pl.BlockSpec((1, N_MOD, d), lambda b, i, j: (b, 0, 0)),
                  pl.BlockSpec((1, d), lambda b, i, j: (0, 0)),
                  pl.BlockSpec((d, tn), lambda b, i, j: (0, j)),
                  pl.BlockSpec((d, nba), lambda b, i, j: (0, 0))],
        out_specs=[pl.BlockSpec((1, tm, tn), lambda b, i, j: (b, i, j)),
                   pl.BlockSpec((1, tm, nba), lambda b, i, j: (b, i, 0))],
        out_shape=[jax.ShapeDtypeStruct((bsz, s, n), BF16),
                   jax.ShapeDtypeStruct((bsz, s, nba), F32)],
        scratch_shapes=[pltpu.VMEM((tm, d), BF16)],
        compiler_params=_cparams(("arbitrary", "arbitrary", "arbitrary")),
        name="inproj",
    )(x, mod, norm_w.reshape(1, d), w1, wba)


def _attn_kernel(qkv_ref, qw_ref, kw_ref, o_ref, lse_ref, qn_ref, kn_ref, vn_ref, *rm_refs, dil, length):
    seq = dil * length
    nb = length // A_SPAN
    if dil > 1:
        stage_ref, orm_ref, lrm_ref = rm_refs
    qw = qw_ref[...] * (HEAD_DIM ** -0.5)
    kw = kw_ref[...]

    def to_residue_major(x, dst_ref, row_off, c0):
        if dil == 1:
            dst_ref[row_off:row_off + seq, c0:c0 + HEAD_DIM] = x.astype(BF16)
        else:
            stage_ref[...] = x
            for r in range(dil):
                lo = row_off + r * length
                dst_ref[lo:lo + length, c0:c0 + HEAD_DIM] = (
                    stage_ref[pl.ds(r, length, stride=dil), :].astype(BF16))

    ones_sq = jnp.ones((HEAD_DIM, HEAD_DIM), BF16)

    def mean_sq(x):
        return jnp.dot((x * x).astype(BF16), ones_sq, preferred_element_type=F32) * (1.0 / HEAD_DIM)

    kn_ref[0:A_SPAN, :] = jnp.zeros((A_SPAN, A_GROUP_WIDTH), BF16)
    vn_ref[0:A_SPAN, :] = jnp.zeros((A_SPAN, A_GROUP_WIDTH), BF16)
    for hd in range(A_HEADS_PER_GROUP):
        c0 = hd * HEAD_DIM
        q = qkv_ref[0, :, c0:c0 + HEAD_DIM].astype(F32)
        k = qkv_ref[0, :, A_GROUP_WIDTH + c0:A_GROUP_WIDTH + c0 + HEAD_DIM].astype(F32)
        v = qkv_ref[0, :, 2 * A_GROUP_WIDTH + c0:2 * A_GROUP_WIDTH + c0 + HEAD_DIM].astype(F32)
        qn = q * lax.rsqrt(mean_sq(q) + NORM_EPS) * qw
        kn = k * lax.rsqrt(mean_sq(k) + NORM_EPS) * kw
        to_residue_major(qn, qn_ref, 0, c0)
        to_residue_major(kn, kn_ref, A_SPAN, c0)
        to_residue_major(v, vn_ref, A_SPAN, c0)

    qi = lax.broadcasted_iota(jnp.int32, (A_SPAN, 2 * A_SPAN), 0)
    kj = lax.broadcasted_iota(jnp.int32, (A_SPAN, 2 * A_SPAN), 1)
    dist = qi + A_SPAN - kj
    band = (dist >= 0) & (dist <= A_SPAN)
    lane = lax.broadcasted_iota(jnp.int32, (A_SPAN, LANES), 1)

    def body(it, carry):
        blocks = [it * A_BLOCKS_PER_ITER + i for i in range(A_BLOCKS_PER_ITER)]
        rows = [pl.multiple_of(bi * A_SPAN, A_SPAN) for bi in blocks]
        masks = [band & (kj >= jnp.where((bi & (nb - 1)) == 0, A_SPAN, 0)) for bi in blocks]
        cols = [slice(hd * HEAD_DIM, (hd + 1) * HEAD_DIM) for hd in range(A_HEADS_PER_GROUP)]
        work = [(i, hd) for i in range(A_BLOCKS_PER_ITER) for hd in range(A_HEADS_PER_GROUP)]
        s = [lax.dot_general(qn_ref[pl.ds(rows[i], A_SPAN), cols[hd]], kn_ref[pl.ds(rows[i], 2 * A_SPAN), cols[hd]],
                             _NT, preferred_element_type=F32) for i, hd in work]
        s = [jnp.where(masks[i], s[n], -jnp.inf) for n, (i, hd) in enumerate(work)]
        m = [jnp.max(x, axis=-1, keepdims=True) for x in s]
        p = [jnp.exp(s[n] - m[n]) for n in range(len(work))]
        den = [jnp.sum(x, axis=-1, keepdims=True) for x in p]
        pv = [jnp.dot(p[n].astype(BF16), vn_ref[pl.ds(rows[i], 2 * A_SPAN), cols[hd]],
                      preferred_element_type=F32) for n, (i, hd) in enumerate(work)]
        lse_tiles = [jnp.zeros((A_SPAN, LANES), F32) for _ in blocks]
        for n, (i, hd) in enumerate(work):
            o = pv[n] / den[n]
            if dil == 1:
                o_ref[0, pl.ds(rows[i], A_SPAN), cols[hd]] = o.astype(BF16)
            else:
                orm_ref[pl.ds(rows[i], A_SPAN), cols[hd]] = o
            lse_tiles[i] = jnp.where(lane == hd, m[n] + jnp.log(den[n]), lse_tiles[i])
        for i in range(A_BLOCKS_PER_ITER):
            if dil == 1:
                lse_ref[0, pl.ds(rows[i], A_SPAN), :] = lse_tiles[i]
            else:
                lrm_ref[pl.ds(rows[i], A_SPAN), :] = lse_tiles[i]
        return carry

    lax.fori_loop(0, seq // (A_SPAN * A_BLOCKS_PER_ITER), body, 0)

    if dil > 1:
        for hd in range(A_HEADS_PER_GROUP):
            c0 = hd * HEAD_DIM
            for r in range(dil):
                stage_ref[pl.ds(r, length, stride=dil), :] = orm_ref[r * length:(r + 1) * length, c0:c0 + HEAD_DIM]
            o_ref[0, :, c0:c0 + HEAD_DIM] = stage_ref[...].astype(BF16)
        for r in range(dil):
            stage_ref[pl.ds(r, length, stride=dil), :] = lrm_ref[r * length:(r + 1) * length, :]
        lse_ref[0] = stage_ref[...]


def _attention_group(proj, g, qw, kw):
    bsz, s, _ = proj.shape
    _, dil = A_GROUPS[g]
    length = s // dil
    col_blk = ATTN_COL0 // A_GROUP_COLS + g
    scratch = [pltpu.VMEM((s, A_GROUP_WIDTH), BF16),
               pltpu.VMEM((s + A_SPAN, A_GROUP_WIDTH), BF16),
               pltpu.VMEM((s + A_SPAN, A_GROUP_WIDTH), BF16)]
    if dil > 1:
        scratch += [pltpu.VMEM((s, HEAD_DIM), F32),
                    pltpu.VMEM((s, A_GROUP_WIDTH), F32),
                    pltpu.VMEM((s, LANES), F32)]
    return pl.pallas_call(
        functools.partial(_attn_kernel, dil=dil, length=length),
        grid=(bsz,),
        in_specs=[pl.BlockSpec((1, s, A_GROUP_COLS), lambda b: (b, 0, col_blk)),
                  pl.BlockSpec((1, HEAD_DIM), lambda b: (0, 0)),
                  pl.BlockSpec((1, HEAD_DIM), lambda b: (0, 0))],
        out_specs=[pl.BlockSpec((1, s, A_GROUP_WIDTH), lambda b: (b, 0, 0)),
                   pl.BlockSpec((1, s, LANES), lambda b: (b, 0, 0))],
        out_shape=[jax.ShapeDtypeStruct((bsz, s, A_GROUP_WIDTH), BF16),
                   jax.ShapeDtypeStruct((bsz, s, LANES), F32)],
        scratch_shapes=scratch,
        compiler_params=_cparams(("arbitrary",)),
        name=f"attn{g}",
    )(proj, qw.reshape(1, HEAD_DIM), kw.reshape(1, HEAD_DIM))


def _split3_bf16(x):
    hi = x.astype(BF16)
    r1 = x - hi.astype(F32)
    mid = r1.astype(BF16)
    lo = (r1 - mid.astype(F32)).astype(BF16)
    return hi, mid, lo


def _bdot(a, b):
    return jnp.einsum("nik,nkj->nij", a, b, preferred_element_type=F32)


def _gdn_kernel(proj_ref, ba_ref, convw_ref, alog_ref, dtb_ref, onw_ref, y_ref,
                mc_ref, ku_ref, iu_ref, egl_ref, st_ref, *, hb, seq):
    c = B_CHUNK
    nc = seq // c
    row8 = lax.broadcasted_iota(jnp.int32, (8, LANES), 0)

    ba = ba_ref[0]
    beta_all = _sigmoid(ba)
    xs = ba + dtb_ref[0]
    softplus = jnp.maximum(xs, 0.0) + jnp.log(1.0 + jnp.exp(-jnp.abs(xs)))
    g_all = -jnp.exp(alog_ref[0]) * softplus

    ii = lax.broadcasted_iota(jnp.int32, (c, c), 0)
    jj = lax.broadcasted_iota(jnp.int32, (c, c), 1)
    causal = (ii >= jj)[None]
    strict = (ii > jj)[None]
    strict_f = (ii > jj).astype(F32)[None]
    tril3 = jnp.broadcast_to(
        jnp.concatenate([(ii >= jj).astype(BF16)] * 3, axis=1)[None], (nc, c, 3 * c))
    ones_sq = jnp.ones((HEAD_DIM, HEAD_DIM), BF16)

    gc3 = _bdot(tril3, jnp.concatenate(_split3_bf16(g_all.reshape(nc, c, LANES)), axis=1))
    gc = gc3.reshape(seq, LANES)
    glast = jnp.broadcast_to(gc3[:, c - 1:c, :], (nc, c, LANES)).reshape(seq, LANES)
    e_gc = jnp.exp(gc)
    e_rem = jnp.exp(glast - gc)
    e_gl = jnp.exp(glast)

    def row_sumsq(x):
        return jnp.dot((x * x).astype(BF16), ones_sq, preferred_element_type=F32)

    def conv_silu(col0):
        x = proj_ref[0, :, col0:col0 + HEAD_DIM].astype(F32)
        cw = convw_ref[:, col0:col0 + HEAD_DIM]
        acc = x * cw[B_CONV - 1:B_CONV]
        for sh in range(1, B_CONV):
            rolled = pltpu.roll(x, sh, axis=0)
            head = jnp.where(row8 >= sh, rolled[0:8], 0.0)
            xs_ = jnp.concatenate([head, rolled[8:]], axis=0)
            acc = acc + xs_ * cw[B_CONV - 1 - sh:B_CONV - sh]
        return _silu(acc)

    for j in range(hb):
        c0 = j * B_HEAD_COLS
        q = conv_silu(c0)
        k = conv_silu(c0 + HEAD_DIM)
        v = conv_silu(c0 + 2 * HEAD_DIM)
        q = q * lax.rsqrt(row_sumsq(q) + NORM_EPS) * (HEAD_DIM ** -0.5)
        k = k * lax.rsqrt(row_sumsq(k) + NORM_EPS)
        a_lane = GDN_A_LANE + j
        beta = beta_all[:, j:j + 1]
        g_col = g_all[:, a_lane:a_lane + 1]
        egc = e_gc[:, a_lane:a_lane + 1]
        kb = k * beta
        k3 = k.astype(BF16).reshape(nc, c, HEAD_DIM)
        kb3 = kb.astype(BF16).reshape(nc, c, HEAD_DIM)
        q3 = q.astype(BF16).reshape(nc, c, HEAD_DIM)
        kk = jnp.einsum("nik,njk->nij", kb3, k3, preferred_element_type=F32)
        qk = jnp.einsum("nik,njk->nij", q3, k3, preferred_element_type=F32)
        bm = g_col.reshape(nc, c, 1) * strict_f
        diff = _bdot(tril3, jnp.concatenate(_split3_bf16(bm), axis=1))
        decay = jnp.exp(diff)
        lower = jnp.where(strict, kk * decay, 0.0)
        intra = jnp.where(causal, qk * decay, 0.0)
        m_pow = -lower
        n_acc = m_pow
        for _ in range(5):
            mb = m_pow.astype(BF16)
            m_pow = _bdot(mb, mb)
            n_acc = n_acc + m_pow + _bdot(n_acc.astype(BF16), m_pow.astype(BF16))
        rhs = jnp.concatenate([v * beta, kb * egc], axis=-1).reshape(nc, c, 2 * HEAD_DIM)
        sol = rhs + _bdot(n_acc.astype(BF16), rhs.astype(BF16))
        u3 = sol[:, :, :HEAD_DIM].astype(BF16)
        w3 = sol[:, :, HEAD_DIM:].astype(BF16)
        ke3 = (k * e_rem[:, a_lane:a_lane + 1]).reshape(nc, c, HEAD_DIM)
        ket3 = jnp.swapaxes(ke3, 1, 2).astype(BF16)
        intra3 = intra.astype(BF16)
        kw_ = _bdot(ket3, w3)
        qw_ = (q * egc).reshape(nc, c, HEAD_DIM) - _bdot(intra3, w3)
        mc_ref[j] = jnp.concatenate([kw_, qw_], axis=1).reshape(nc * (HEAD_DIM + c), HEAD_DIM).astype(BF16)
        ku_ref[j] = _bdot(ket3, u3).reshape(nc * HEAD_DIM, HEAD_DIM)
        iu_ref[j] = _bdot(intra3, u3).reshape(seq, HEAD_DIM).astype(BF16)
        egl_col = e_gl[:, a_lane:a_lane + 1].reshape(nc, c, 1)[:, 0, :]
        egl_ref[j] = jnp.broadcast_to(egl_col, (nc, LANES))

    st_ref[...] = jnp.zeros_like(st_ref)
    onw = onw_ref[...]

    def body(n, carry):
        r0 = pl.multiple_of(n * c, c)
        k0 = pl.multiple_of(n * HEAD_DIM, HEAD_DIM)
        m0 = pl.multiple_of(n * (HEAD_DIM + c), c)
        heads = range(hb)
        state = [st_ref[j] for j in heads]
        ms = [jnp.dot(mc_ref[j, pl.ds(m0, HEAD_DIM + c), :], state[j].astype(BF16),
                      preferred_element_type=F32) for j in heads]
        for j in heads:
            st_ref[j] = (state[j] * egl_ref[j, pl.ds(n, 1), :] - ms[j][:HEAD_DIM]
                         + ku_ref[j, pl.ds(k0, HEAD_DIM), :])
        o = [ms[j][HEAD_DIM:] + iu_ref[j, pl.ds(r0, c), :].astype(F32) for j in heads]
        ss = [row_sumsq(o[j]) for j in heads]
        for j in heads:
            zc = j * B_HEAD_COLS + 3 * HEAD_DIM
            z = proj_ref[0, pl.ds(r0, c), zc:zc + HEAD_DIM].astype(F32)
            on = o[j] * lax.rsqrt(ss[j] * (1.0 / HEAD_DIM) + NORM_EPS) * onw
            y_ref[0, pl.ds(r0, c), j * HEAD_DIM:(j + 1) * HEAD_DIM] = (on * _silu(z)).astype(BF16)
        return carry

    lax.fori_loop(0, nc, body, 0, unroll=2)


def _gdn(proj, ba, convw, alog, dtb, onw, hb=GDN_HEADS_PER_STEP):
    bsz, s, _ = proj.shape
    ng = B_HEADS // hb
    nc = s // B_CHUNK
    return pl.pallas_call(
        functools.partial(_gdn_kernel, hb=hb, seq=s),
        grid=(bsz, ng),
        in_specs=[pl.BlockSpec((1, s, hb * B_HEAD_COLS), lambda b, g: (b, 0, g)),
                  pl.BlockSpec((1, s, LANES), lambda b, g: (b, 0, g)),
                  pl.BlockSpec((B_CONV, hb * B_HEAD_COLS), lambda b, g: (0, g)),
                  pl.BlockSpec((1, 1, LANES), lambda b, g: (g, 0, 0)),
                  pl.BlockSpec((1, 1, LANES), lambda b, g: (g, 0, 0)),
                  pl.BlockSpec((1, HEAD_DIM), lambda b, g: (0, 0))],
        out_specs=pl.BlockSpec((1, s, hb * HEAD_DIM), lambda b, g: (b, 0, g)),
        out_shape=jax.ShapeDtypeStruct((bsz, s, B_HEADS * HEAD_DIM), BF16),
        scratch_shapes=[pltpu.VMEM((hb, nc * (HEAD_DIM + B_CHUNK), HEAD_DIM), BF16),
                        pltpu.VMEM((hb, nc * HEAD_DIM, HEAD_DIM), F32),
                        pltpu.VMEM((hb, s, HEAD_DIM), BF16),
                        pltpu.VMEM((hb, nc, LANES), F32),
                        pltpu.VMEM((hb, HEAD_DIM, HEAD_DIM), F32)],
        compiler_params=_cparams(("arbitrary", "arbitrary")),
        name="gdn",
    )(proj, ba, convw, alog, dtb, onw.reshape(1, HEAD_DIM))


def _mix_kernel(o0_ref, o1_ref, o2_ref, l0_ref, l1_ref, l2_ref, yb_ref, ga_ref, gb_ref, x_ref,
                mod_ref, wa_ref, wb_ref, wo_ref, n2_ref, wr_ref, br_ref,
                x1_ref, h2_ref, route_ref):
    tm = x_ref.shape[1]
    o_refs = (o0_ref, o1_ref, o2_ref)
    ls = [l0_ref[0], l1_ref[0], l2_ref[0]]
    mx = jnp.maximum(jnp.maximum(ls[0], ls[1]), ls[2])
    es = [jnp.exp(l - mx) for l in ls]
    den = es[0] + es[1] + es[2]
    ws = [e / den for e in es]
    heads = []
    for hd in range(A_HEADS_PER_GROUP):
        c0 = hd * HEAD_DIM
        acc = ws[0][:, hd:hd + 1] * o_refs[0][0, :, c0:c0 + HEAD_DIM].astype(F32)
        for g in (1, 2):
            acc = acc + ws[g][:, hd:hd + 1] * o_refs[g][0, :, c0:c0 + HEAD_DIM].astype(F32)
        heads.append(acc)
    ya = jnp.concatenate(heads, axis=1).astype(BF16)
    a = jnp.dot(ya, wa_ref[...], preferred_element_type=F32)
    bm = jnp.dot(yb_ref[0], wb_ref[...], preferred_element_type=F32)
    merged = _sigmoid(ga_ref[0].astype(F32)) * a + _sigmoid(gb_ref[0].astype(F32)) * bm
    mix = jnp.dot(merged.astype(BF16), wo_ref[...], preferred_element_type=F32)
    m = mod_ref[0]
    x1 = x_ref[0] + m[2:3] * mix
    x1_ref[0] = x1
    h2 = _modulated_norm(x1, n2_ref[...], m[4:5], m[3:4])
    h2_ref[0] = h2.astype(BF16)

    lt = lax.dot_general(wr_ref[...], h2, _NT, preferred_element_type=F32,
                         precision=lax.Precision.HIGHEST) + br_ref[...]
    gl = [lt[i:i + 1] for i in range(MOE_GROUPS)]
    gm = jnp.maximum(jnp.maximum(gl[0], gl[1]), jnp.maximum(gl[2], gl[3]))
    gsum = sum(jnp.exp(x - gm) for x in gl)
    p_top = 1.0 / gsum
    sel, taken = [], jnp.zeros(gm.shape, jnp.bool_)
    for i in range(MOE_GROUPS):
        s_i = jnp.logical_and(gl[i] == gm, jnp.logical_not(taken))
        taken = jnp.logical_or(taken, s_i)
        sel.append(s_i)
    ig = []
    for e in range(MOE_EXPERTS_PER_GROUP):
        acc = jnp.zeros(gm.shape, F32)
        for g in range(MOE_GROUPS):
            r = ROUTER_EXPERT_ROW0 + g * MOE_EXPERTS_PER_GROUP + e
            acc = jnp.where(sel[g], lt[r:r + 1], acc)
        ig.append(acc)
    t1 = jnp.maximum(jnp.maximum(ig[0], ig[1]), jnp.maximum(ig[2], ig[3]))
    is1, taken = [], jnp.zeros(gm.shape, jnp.bool_)
    for e in range(MOE_EXPERTS_PER_GROUP):
        s_e = jnp.logical_and(ig[e] == t1, jnp.logical_not(taken))
        taken = jnp.logical_or(taken, s_e)
        is1.append(s_e)
    rest = [jnp.where(is1[e], -jnp.inf, ig[e]) for e in range(MOE_EXPERTS_PER_GROUP)]
    t2 = jnp.maximum(jnp.maximum(rest[0], rest[1]), jnp.maximum(rest[2], rest[3]))
    is2, taken = [], jnp.zeros(gm.shape, jnp.bool_)
    for e in range(MOE_EXPERTS_PER_GROUP):
        s_e = jnp.logical_and(rest[e] == t2, jnp.logical_not(taken))
        taken = jnp.logical_or(taken, s_e)
        is2.append(s_e)
    e2 = jnp.exp(t2 - t1)
    w1 = p_top / (1.0 + e2)
    w2 = p_top * e2 / (1.0 + e2)
    wt = [jnp.where(is1[e], w1, 0.0) + jnp.where(is2[e], w2, 0.0) for e in range(MOE_EXPERTS_PER_GROUP)]
    gid = jnp.zeros(gm.shape, F32)
    for g in range(1, MOE_GROUPS):
        gid = jnp.where(sel[g], float(g), gid)
    rowi = lax.broadcasted_iota(jnp.int32, (ROUTE_ROWS, tm), 0)
    slab = jnp.where(rowi == 0, jnp.broadcast_to(gid, (ROUTE_ROWS, tm)), 0.0)
    for e in range(MOE_EXPERTS_PER_GROUP):
        slab = jnp.where(rowi == 1 + e, jnp.broadcast_to(wt[e], (ROUTE_ROWS, tm)), slab)
    route_ref[0] = slab


def _mix(o_list, lse_list, yb, proj, x, mod, wa, wb, wo, n2w, wr, br, tm=512):
    bsz, s, d = x.shape
    tok = lambda b, i: (b, i, 0)
    full = lambda b, i: (0, 0)
    gate_blk = GATE_COL0 // d
    in_specs = ([pl.BlockSpec((1, tm, A_GROUP_WIDTH), tok)] * 3
                + [pl.BlockSpec((1, tm, LANES), tok)] * 3
                + [pl.BlockSpec((1, tm, d), tok),
                   pl.BlockSpec((1, tm, d), lambda b, i: (b, i, gate_blk)),
                   pl.BlockSpec((1, tm, d), lambda b, i: (b, i, gate_blk + 1)),
                   pl.BlockSpec((1, tm, d), tok),
                   pl.BlockSpec((1, N_MOD, d), lambda b, i: (b, 0, 0)),
                   pl.BlockSpec(wa.shape, full),
                   pl.BlockSpec(wb.shape, full),
                   pl.BlockSpec(wo.shape, full),
                   pl.BlockSpec((1, d), full),
                   pl.BlockSpec(wr.shape, full),
                   pl.BlockSpec(br.shape, full)])
    return pl.pallas_call(
        _mix_kernel,
        grid=(bsz, s // tm),
        in_specs=in_specs,
        out_specs=[pl.BlockSpec((1, tm, d), tok),
                   pl.BlockSpec((1, tm, d), tok),
                   pl.BlockSpec((1, ROUTE_ROWS, tm), lambda b, i: (b, 0, i))],
        out_shape=[jax.ShapeDtypeStruct((bsz, s, d), F32),
                   jax.ShapeDtypeStruct((bsz, s, d), BF16),
                   jax.ShapeDtypeStruct((bsz, ROUTE_ROWS, s), F32)],
        compiler_params=_cparams(("arbitrary", "arbitrary")),
        name="mix",
    )(*o_list, *lse_list, yb, proj, proj, x, mod, wa, wb, wo, n2w.reshape(1, d), wr, br)


def _moe_kernel(h2_ref, route_ref, wg_ref, wu_ref, wd_ref, x1_ref, mod_ref, out_ref, xa_ref, ys_ref):
    tm, blk = MOE_TILE, MOE_BLOCK
    d = h2_ref.shape[1]
    n_slots = ys_ref.shape[0]
    r = route_ref[0]
    rowi = lax.broadcasted_iota(jnp.int32, (ROUTE_ROWS, tm), 0)
    gid = jnp.broadcast_to(r[0:1], (ROUTE_ROWS, tm))
    onehot = jnp.where((gid == rowi.astype(F32)) & (rowi < MOE_GROUPS), 1.0, 0.0)
    oh_b = onehot.astype(BF16)
    kk = lax.broadcasted_iota(jnp.int32, (tm, tm), 0)
    tt = lax.broadcasted_iota(jnp.int32, (tm, tm), 1)
    before = jnp.dot(oh_b, (kk < tt).astype(BF16), preferred_element_type=F32)
    count = jnp.dot(oh_b, jnp.ones((tm, tm), BF16), preferred_element_type=F32)
    nblk = jnp.floor((count + (blk - 1)) * (1.0 / blk))
    slot = jnp.zeros((1, tm), F32)
    base = jnp.zeros((1, tm), F32)
    for g in range(MOE_GROUPS):
        slot = slot + onehot[g:g + 1] * (base + before[g:g + 1])
        base = base + nblk[g:g + 1] * blk
    slot_i = slot.astype(jnp.int32)

    slab = jnp.where(rowi == 0, jnp.broadcast_to(slot, (ROUTE_ROWS, tm)), r)
    slab_t = jnp.concatenate([slab, jnp.zeros((LANES - ROUTE_ROWS, tm), F32)], axis=0).T
    slot_col = jnp.broadcast_to(slab_t[:, 0:1], (tm, LANES)).astype(jnp.int32)
    cw_hi = slab_t.astype(BF16)
    cw_lo = (slab_t - cw_hi.astype(F32)).astype(BF16)
    xa_ref[:, 0:d] = h2_ref[...]
    xa_ref[:, d:d + LANES] = cw_hi
    xa_ref[:, d + LANES:d + 2 * LANES] = cw_lo

    done = jnp.int32(0)
    for g in range(MOE_GROUPS):
        n_g = (jnp.sum(onehot[g:g + 1]).astype(jnp.int32) + (blk - 1)) // blk

        def body(j, carry, g=g, first=done):
            s0 = pl.multiple_of((first + j) * blk, MOE_ROW_ALIGN)
            rows = lax.broadcasted_iota(jnp.int32, (blk, tm), 0) + s0
            gather = (rows == jnp.broadcast_to(slot_i, (blk, tm))).astype(BF16)
            xb = jnp.dot(gather, xa_ref[...], preferred_element_type=F32)
            hb = xb[:, 0:d].astype(BF16)
            cwb = xb[:, d:d + LANES] + xb[:, d + LANES:d + 2 * LANES]
            experts = range(g * MOE_EXPERTS_PER_GROUP, (g + 1) * MOE_EXPERTS_PER_GROUP)
            gate = [jnp.dot(hb, wg_ref[e], preferred_element_type=F32) for e in experts]
            up = [jnp.dot(hb, wu_ref[e], preferred_element_type=F32) for e in experts]
            hid = [(_silu(gate[i]) * up[i] * cwb[:, 1 + i:2 + i]).astype(BF16)
                   for i in range(MOE_EXPERTS_PER_GROUP)]
            y = jnp.dot(hid[0], wd_ref[experts[0]], preferred_element_type=F32)
            for i in range(1, MOE_EXPERTS_PER_GROUP):
                y = y + jnp.dot(hid[i], wd_ref[experts[i]], preferred_element_type=F32)
            ys_ref[pl.ds(s0, blk), :] = y.astype(BF16)
            return carry

        lax.fori_loop(0, n_g, body, 0)
        done = done + n_g

    def zero_block(j, carry):
        ys_ref[pl.ds(pl.multiple_of(j * blk, MOE_ROW_ALIGN), blk), :] = jnp.zeros((blk, d), BF16)
        return carry

    max_blocks = tm // blk + MOE_GROUPS
    lax.fori_loop(done, max_blocks, zero_block, 0)
    if n_slots > max_blocks * blk:
        ys_ref[max_blocks * blk:n_slots, :] = jnp.zeros((n_slots - max_blocks * blk, d), BF16)
    cols = lax.broadcasted_iota(jnp.int32, (tm, LANES), 1)
    scatter = jnp.concatenate([(cols + c0 == slot_col).astype(BF16) for c0 in range(0, n_slots, LANES)], axis=1)
    moe = jnp.dot(scatter, ys_ref[...], preferred_element_type=F32)
    out_ref[...] = x1_ref[...] + mod_ref[0][5:6] * moe


def _moe(h2, route, wg, wu, wd, x1, mod, seq):
    t, d = h2.shape
    tm = MOE_TILE
    n_slots = -(-((tm // MOE_BLOCK + MOE_GROUPS) * MOE_BLOCK) // LANES) * LANES
    tiles_per_seq = seq // tm
    resident = lambda shape: pl.BlockSpec(shape, lambda i: (0, 0, 0), pipeline_mode=pl.Buffered(1))
    return pl.pallas_call(
        _moe_kernel,
        grid=(t // tm,),
        in_specs=[pl.BlockSpec((tm, d), lambda i: (i, 0)),
                  pl.BlockSpec((1, ROUTE_ROWS, tm), lambda i: (i // tiles_per_seq, 0, i % tiles_per_seq)),
                  resident(wg.shape), resident(wu.shape), resident(wd.shape),
                  pl.BlockSpec((tm, d), lambda i: (i, 0)),
                  pl.BlockSpec((1, N_MOD, d), lambda i: (i // tiles_per_seq, 0, 0))],
        out_specs=pl.BlockSpec((tm, d), lambda i: (i, 0)),
        out_shape=jax.ShapeDtypeStruct((t, d), F32),
        scratch_shapes=[pltpu.VMEM((tm, d + 2 * LANES), BF16),
                        pltpu.VMEM((n_slots, d), BF16)],
        compiler_params=_cparams(("arbitrary",)),
        name="moe",
    )(h2, route, wg, wu, wd, x1, mod)


def _prep_in_weights(w_in, conv_w, a_log, dt_bias, hb):
    d = w_in.shape[0]
    aw = 3 * A_GROUP_WIDTH
    bw = B_HEADS * HEAD_DIM
    offs = np.cumsum([0, aw, aw, aw, bw, bw, bw, bw, B_HEADS, B_HEADS, d, d])
    qa, ka, va, qb, kb, vb, zb, beta, a_raw, ga, gb = [w_in[:, offs[i]:offs[i + 1]] for i in range(11)]
    cols, conv_cols = [], []
    for h in range(B_HEADS):
        sl = slice(h * HEAD_DIM, (h + 1) * HEAD_DIM)
        cols += [qb[:, sl], kb[:, sl], vb[:, sl], zb[:, sl]]
        conv_cols += [conv_w[:, sl], conv_w[:, bw + h * HEAD_DIM:bw + (h + 1) * HEAD_DIM],
                      conv_w[:, 2 * bw + h * HEAD_DIM:2 * bw + (h + 1) * HEAD_DIM],
                      jnp.zeros((B_CONV, HEAD_DIM), conv_w.dtype)]
    cols += [ga, gb]
    for g in range(len(A_GROUPS)):
        sl = slice(g * A_GROUP_WIDTH, (g + 1) * A_GROUP_WIDTH)
        cols += [qa[:, sl], ka[:, sl], va[:, sl]]
    w1 = jnp.concatenate(cols, axis=1).astype(BF16)
    convw = jnp.concatenate(conv_cols, axis=1)
    ng = B_HEADS // hb
    pad = jnp.zeros((d, LANES - GDN_A_LANE - hb), w_in.dtype)
    gap = jnp.zeros((d, GDN_A_LANE - hb), w_in.dtype)
    ba_cols = []
    vec_pad = lambda v: jnp.concatenate([jnp.zeros((GDN_A_LANE,), F32), v, jnp.zeros((LANES - GDN_A_LANE - hb,), F32)])
    alog, dtb = [], []
    for gi in range(ng):
        sl = slice(gi * hb, (gi + 1) * hb)
        ba_cols += [beta[:, sl], gap, a_raw[:, sl], pad]
        alog.append(vec_pad(a_log[sl].astype(F32)))
        dtb.append(vec_pad(dt_bias[sl].astype(F32)))
    wba = jnp.concatenate(ba_cols, axis=1).astype(BF16)
    return w1, wba, convw, jnp.stack(alog)[:, None, :], jnp.stack(dtb)[:, None, :]


def _prep_router(w_rg, b_rg, w_re, b_re):
    d = w_rg.shape[0]
    n_exp = MOE_GROUPS * MOE_EXPERTS_PER_GROUP
    wr = jnp.concatenate([w_rg.T, jnp.zeros((ROUTER_EXPERT_ROW0 - MOE_GROUPS, d), F32), w_re.T,
                          jnp.zeros((ROUTER_ROWS - ROUTER_EXPERT_ROW0 - n_exp, d), F32)], axis=0)
    br = jnp.concatenate([b_rg, jnp.zeros((ROUTER_EXPERT_ROW0 - MOE_GROUPS,), F32), b_re.reshape(-1),
                          jnp.zeros((ROUTER_ROWS - ROUTER_EXPERT_ROW0 - n_exp,), F32)])
    return wr.astype(F32), br.astype(F32)[:, None]


def kernel(x, c, w_ada, b_ada, norm1_w, w_in, conv_w, a_q_norm_w, a_k_norm_w, b_A_log, b_dt_bias, b_out_norm_w, w_branch_a, w_branch_b, w_o, norm2_w, w_router_group, b_router_group, w_router_expert, b_router_expert, w_exp_gate, w_exp_up, w_exp_down):
    bsz, s, d = x.shape
    hb = GDN_HEADS_PER_STEP
    for i in range(w_ada.shape[0]):
        mod = _ada(c, w_ada[i], b_ada[i]).reshape(bsz, N_MOD, d)
        w1, wba, convw, alog, dtb = _prep_in_weights(w_in[i], conv_w[i], b_A_log[i], b_dt_bias[i], hb)
        proj, ba = _inproj(x, mod, norm1_w[i], w1, wba)
        o_list, lse_list = [], []
        for g in range(len(A_GROUPS)):
            o_g, lse_g = _attention_group(proj, g, a_q_norm_w[i], a_k_norm_w[i])
            o_list.append(o_g)
            lse_list.append(lse_g)
        yb = _gdn(proj, ba, convw, alog, dtb, b_out_norm_w[i], hb)
        wr, br = _prep_router(w_router_group[i], b_router_group[i], w_router_expert[i], b_router_expert[i])
        x1, h2, route = _mix(o_list, lse_list, yb, proj, x, mod, w_branch_a[i].astype(BF16),
                          w_branch_b[i].astype(BF16), w_o[i].astype(BF16), norm2_w[i], wr, br)
        out = _moe(h2.reshape(bsz * s, d), route, w_exp_gate[i].astype(BF16), w_exp_up[i].astype(BF16),
                   w_exp_down[i].astype(BF16), x1.reshape(bsz * s, d), mod, s)
        x = out.reshape(bsz, s, d)
    return x
```

```python
import functools

import jax
import jax.numpy as jnp
import numpy as np
from jax import lax
from jax.experimental import pallas as pl
from jax.experimental.pallas import tpu as pltpu

F32 = jnp.float32
BF16 = jnp.bfloat16

NORM_EPS = 1e-6
N_MOD = 6

A_GROUPS = ((128, 1), (512, 4), (2048, 16))
A_HEADS_PER_GROUP = 4
HEAD_DIM = 128
A_GROUP_WIDTH = A_HEADS_PER_GROUP * HEAD_DIM
A_GROUP_COLS = 3 * A_GROUP_WIDTH
A_SPAN = 128
A_BLOCKS_PER_ITER = 4

B_HEADS = 8
B_CONV = 4
B_CHUNK = 64
B_HEAD_COLS = 4 * HEAD_DIM
GDN_HEADS_PER_STEP = 4
GDN_A_LANE = 8

MOE_GROUPS = 4
MOE_EXPERTS_PER_GROUP = 4
MOE_D_FF = 256
ROUTER_ROWS = 32
ROUTER_EXPERT_ROW0 = 8
ROUTE_ROWS = 8
MOE_TILE = 512
MOE_BLOCK = 160
MOE_ROW_ALIGN = 16

GDN_COLS = B_HEADS * B_HEAD_COLS
GATE_COL0 = GDN_COLS
ATTN_COL0 = GDN_COLS + 2048
PROJ_COLS = ATTN_COL0 + 3 * A_GROUP_COLS

LANES = 128
VMEM_LIMIT_BYTES = 56 * 1024 * 1024

_NT = (((1,), (1,)), ((), ()))
_TN = (((0,), (0,)), ((), ()))


def _sigmoid(x):
    return 0.5 * jnp.tanh(0.5 * x) + 0.5


def _silu(x):
    return x * _sigmoid(x)


def _cparams(sem):
    return pltpu.CompilerParams(dimension_semantics=sem, vmem_limit_bytes=VMEM_LIMIT_BYTES)


def _ada_kernel(c_ref, w_ref, b_ref, o_ref):
    c = c_ref[...]
    o_ref[...] = jnp.dot(_silu(c), w_ref[...], preferred_element_type=F32,
                         precision=lax.Precision.HIGHEST) + b_ref[...]


def _ada(c, w_ada, b_ada):
    bsz, d = c.shape
    n = w_ada.shape[1]
    tn = 1536
    return pl.pallas_call(
        _ada_kernel,
        grid=(n // tn,),
        in_specs=[pl.BlockSpec((bsz, d), lambda j: (0, 0)),
                  pl.BlockSpec((d, tn), lambda j: (0, j)),
                  pl.BlockSpec((1, tn), lambda j: (0, j))],
        out_specs=pl.BlockSpec((bsz, tn), lambda j: (0, j)),
        out_shape=jax.ShapeDtypeStruct((bsz, n), F32),
        compiler_params=_cparams(("arbitrary",)),
        name="ada",
    )(c, w_ada, b_ada.reshape(1, n))


def _modulated_norm(x, norm_w, scale, shift):
    y = x * lax.rsqrt(jnp.mean(x * x, axis=-1, keepdims=True) + NORM_EPS)
    return (y * norm_w) * (1.0 + scale) + shift


def _inproj_kernel(x_ref, mod_ref, nw_ref, w_ref, wba_ref, proj_ref, ba_ref, h_ref):
    @pl.when(pl.program_id(2) == 0)
    def _():
        m = mod_ref[0]
        h = _modulated_norm(x_ref[0], nw_ref[...], m[1:2], m[0:1]).astype(BF16)
        h_ref[...] = h
        ba_ref[0] = jnp.dot(h, wba_ref[...], preferred_element_type=F32)

    proj_ref[0] = jnp.dot(h_ref[...], w_ref[...], preferred_element_type=F32).astype(BF16)


def _inproj(x, mod, norm_w, w1, wba, tm=1024, tn=1536):
    bsz, s, d = x.shape
    n = w1.shape[1]
    nba = wba.shape[1]
    return pl.pallas_call(
        _inproj_kernel,
        grid=(bsz, s // tm, n // tn),
        in_specs=[pl.BlockSpec((1, tm, d), lambda b, i, j: (b, i, 0)),
                  pl.BlockSpec((1, N_MOD, d), lambda b, i, j: (b, 0, 0)),
                  pl.BlockSpec((1, d), lambda b, i, j: (0, 0)),
                  pl.BlockSpec((d, tn), lambda b, i, j: (0, j)),
                  pl.BlockSpec((d, nba), lambda b, i, j: (0, 0))],
        out_specs=[pl.BlockSpec((1, tm, tn), lambda b, i, j: (b, i, j)),
                   pl.BlockSpec((1, tm, nba), lambda b, i, j: (b, i, 0))],
        out_shape=[jax.ShapeDtypeStruct((bsz, s, n), BF16),
                   jax.ShapeDtypeStruct((bsz, s, nba), F32)],
        scratch_shapes=[pltpu.VMEM((tm, d), BF16)],
        compiler_params=_cparams(("arbitrary", "arbitrary", "arbitrary")),
        name="inproj",
    )(x, mod, norm_w.reshape(1, d), w1, wba)


def _attn_kernel(qkv_ref, qw_ref, kw_ref, o_ref, lse_ref, qn_ref, kn_ref, vn_ref, *rm_refs, dil, length):
    seq = dil * length
    nb = length // A_SPAN
    if dil > 1:
        stage_ref, orm_ref, lrm_ref = rm_refs
    qw = qw_ref[...] * (HEAD_DIM ** -0.5)
    kw = kw_ref[...]

    def to_residue_major(x, dst_ref, row_off, c0):
        if dil == 1:
            dst_ref[row_off:row_off + seq, c0:c0 + HEAD_DIM] = x.astype(BF16)
        else:
            stage_ref[...] = x
            for r in range(dil):
                lo = row_off + r * length
                dst_ref[lo:lo + length, c0:c0 + HEAD_DIM] = (
                    stage_ref[pl.ds(r, length, stride=dil), :].astype(BF16))

    ones_sq = jnp.ones((HEAD_DIM, HEAD_DIM), BF16)

    def mean_sq(x):
        return jnp.dot((x * x).astype(BF16), ones_sq, preferred_element_type=F32) * (1.0 / HEAD_DIM)

    kn_ref[0:A_SPAN, :] = jnp.zeros((A_SPAN, A_GROUP_WIDTH), BF16)
    vn_ref[0:A_SPAN, :] = jnp.zeros((A_SPAN, A_GROUP_WIDTH), BF16)
    for hd in range(A_HEADS_PER_GROUP):
        c0 = hd * HEAD_DIM
        q = qkv_ref[0, :, c0:c0 + HEAD_DIM].astype(F32)
        k = qkv_ref[0, :, A_GROUP_WIDTH + c0:A_GROUP_WIDTH + c0 + HEAD_DIM].astype(F32)
        v = qkv_ref[0, :, 2 * A_GROUP_WIDTH + c0:2 * A_GROUP_WIDTH + c0 + HEAD_DIM].astype(F32)
        qn = q * lax.rsqrt(mean_sq(q) + NORM_EPS) * qw
        kn = k * lax.rsqrt(mean_sq(k) + NORM_EPS) * kw
        to_residue_major(qn, qn_ref, 0, c0)
        to_residue_major(kn, kn_ref, A_SPAN, c0)
        to_residue_major(v, vn_ref, A_SPAN, c0)

    qi = lax.broadcasted_iota(jnp.int32, (A_SPAN, 2 * A_SPAN), 0)
    kj = lax.broadcasted_iota(jnp.int32, (A_SPAN, 2 * A_SPAN), 1)
    dist = qi + A_SPAN - kj
    band = (dist >= 0) & (dist <= A_SPAN)
    lane = lax.broadcasted_iota(jnp.int32, (A_SPAN, LANES), 1)

    def body(it, carry):
        blocks = [it * A_BLOCKS_PER_ITER + i for i in range(A_BLOCKS_PER_ITER)]
        rows = [pl.multiple_of(bi * A_SPAN, A_SPAN) for bi in blocks]
        masks = [band & (kj >= jnp.where((bi & (nb - 1)) == 0, A_SPAN, 0)) for bi in blocks]
        cols = [slice(hd * HEAD_DIM, (hd + 1) * HEAD_DIM) for hd in range(A_HEADS_PER_GROUP)]
        work = [(i, hd) for i in range(A_BLOCKS_PER_ITER) for hd in range(A_HEADS_PER_GROUP)]
        s = [lax.dot_general(qn_ref[pl.ds(rows[i], A_SPAN), cols[hd]], kn_ref[pl.ds(rows[i], 2 * A_SPAN), cols[hd]],
                             _NT, preferred_element_type=F32) for i, hd in work]
        s = [jnp.where(masks[i], s[n], -jnp.inf) for n, (i, hd) in enumerate(work)]
        m = [jnp.max(x, axis=-1, keepdims=True) for x in s]
        p = [jnp.exp(s[n] - m[n]) for n in range(len(work))]
        den = [jnp.sum(x, axis=-1, keepdims=True) for x in p]
        pv = [jnp.dot(p[n].astype(BF16), vn_ref[pl.ds(rows[i], 2 * A_SPAN), cols[hd]],
                      preferred_element_type=F32) for n, (i, hd) in enumerate(work)]
        lse_tiles = [jnp.zeros((A_SPAN, LANES), F32) for _ in blocks]
        for n, (i, hd) in enumerate(work):
            o = pv[n] / den[n]
            if dil == 1:
                o_ref[0, pl.ds(rows[i], A_SPAN), cols[hd]] = o.astype(BF16)
            else:
                orm_ref[pl.ds(rows[i], A_SPAN), cols[hd]] = o
            lse_tiles[i] = jnp.where(lane == hd, m[n] + jnp.log(den[n]), lse_tiles[i])
        for i in range(A_BLOCKS_PER_ITER):
            if dil == 1:
                lse_ref[0, pl.ds(rows[i], A_SPAN), :] = lse_tiles[i]
            else:
                lrm_ref[pl.ds(rows[i], A_SPAN), :] = lse_tiles[i]
        return carry

    lax.fori_loop(0, seq // (A_SPAN * A_BLOCKS_PER_ITER), body, 0)

    if dil > 1:
        for hd in range(A_HEADS_PER_GROUP):
            c0 = hd * HEAD_DIM
            for r in range(dil):
                stage_ref[pl.ds(r, length, stride=dil), :] = orm_ref[r * length:(r + 1) * length, c0:c0 + HEAD_DIM]
            o_ref[0, :, c0:c0 + HEAD_DIM] = stage_ref[...].astype(BF16)
        for r in range(dil):
            stage_ref[pl.ds(r, length, stride=dil), :] = lrm_ref[r * length:(r + 1) * length, :]
        lse_ref[0] = stage_ref[...]


def _attention_group(proj, g, qw, kw):
    bsz, s, _ = proj.shape
    _, dil = A_GROUPS[g]
    length = s // dil
    col_blk = ATTN_COL0 // A_GROUP_COLS + g
    scratch = [pltpu.VMEM((s, A_GROUP_WIDTH), BF16),
               pltpu.VMEM((s + A_SPAN, A_GROUP_WIDTH), BF16),
               pltpu.VMEM((s + A_SPAN, A_GROUP_WIDTH), BF16)]
    if dil > 1:
        scratch += [pltpu.VMEM((s, HEAD_DIM), F32),
                    pltpu.VMEM((s, A_GROUP_WIDTH), F32),
                    pltpu.VMEM((s, LANES), F32)]
    return pl.pallas_call(
        functools.partial(_attn_kernel, dil=dil, length=length),
        grid=(bsz,),
        in_specs=[pl.BlockSpec((1, s, A_GROUP_COLS), lambda b: (b, 0, col_blk)),
                  pl.BlockSpec((1, HEAD_DIM), lambda b: (0, 0)),
                  pl.BlockSpec((1, HEAD_DIM), lambda b: (0, 0))],
        out_specs=[pl.BlockSpec((1, s, A_GROUP_WIDTH), lambda b: (b, 0, 0)),
                   pl.BlockSpec((1, s, LANES), lambda b: (b, 0, 0))],
        out_shape=[jax.ShapeDtypeStruct((bsz, s, A_GROUP_WIDTH), BF16),
                   jax.ShapeDtypeStruct((bsz, s, LANES), F32)],
        scratch_shapes=scratch,
        compiler_params=_cparams(("arbitrary",)),
        name=f"attn{g}",
    )(proj, qw.reshape(1, HEAD_DIM), kw.reshape(1, HEAD_DIM))


def _split3_bf16(x):
    hi = x.astype(BF16)
    r1 = x - hi.astype(F32)
    mid = r1.astype(BF16)
    lo = (r1 - mid.astype(F32)).astype(BF16)
    return hi, mid, lo


def _bdot(a, b):
    return jnp.einsum("nik,nkj->nij", a, b, preferred_element_type=F32)


def _gdn_kernel(proj_ref, ba_ref, convw_ref, alog_ref, dtb_ref, onw_ref, y_ref,
                mc_ref, ku_ref, iu_ref, egl_ref, st_ref, *, hb, seq):
    c = B_CHUNK
    nc = seq // c
    row8 = lax.broadcasted_iota(jnp.int32, (8, LANES), 0)

    ba = ba_ref[0]
    beta_all = _sigmoid(ba)
    xs = ba + dtb_ref[0]
    softplus = jnp.maximum(xs, 0.0) + jnp.log(1.0 + jnp.exp(-jnp.abs(xs)))
    g_all = -jnp.exp(alog_ref[0]) * softplus

    ii = lax.broadcasted_iota(jnp.int32, (c, c), 0)
    jj = lax.broadcasted_iota(jnp.int32, (c, c), 1)
    causal = (ii >= jj)[None]
    strict = (ii > jj)[None]
    strict_f = (ii > jj).astype(F32)[None]
    tril3 = jnp.broadcast_to(
        jnp.concatenate([(ii >= jj).astype(BF16)] * 3, axis=1)[None], (nc, c, 3 * c))
    ones_sq = jnp.ones((HEAD_DIM, HEAD_DIM), BF16)

    gc3 = _bdot(tril3, jnp.concatenate(_split3_bf16(g_all.reshape(nc, c, LANES)), axis=1))
    gc = gc3.reshape(seq, LANES)
    glast = jnp.broadcast_to(gc3[:, c - 1:c, :], (nc, c, LANES)).reshape(seq, LANES)
    e_gc = jnp.exp(gc)
    e_rem = jnp.exp(glast - gc)
    e_gl = jnp.exp(glast)

    def row_sumsq(x):
        return jnp.dot((x * x).astype(BF16), ones_sq, preferred_element_type=F32)

    def conv_silu(col0):
        x = proj_ref[0, :, col0:col0 + HEAD_DIM].astype(F32)
        cw = convw_ref[:, col0:col0 + HEAD_DIM]
        acc = x * cw[B_CONV - 1:B_CONV]
        for sh in range(1, B_CONV):
            rolled = pltpu.roll(x, sh, axis=0)
            head = jnp.where(row8 >= sh, rolled[0:8], 0.0)
            xs_ = jnp.concatenate([head, rolled[8:]], axis=0)
            acc = acc + xs_ * cw[B_CONV - 1 - sh:B_CONV - sh]
        return _silu(acc)

    for j in range(hb):
        c0 = j * B_HEAD_COLS
        q = conv_silu(c0)
        k = conv_silu(c0 + HEAD_DIM)
        v = conv_silu(c0 + 2 * HEAD_DIM)
        q = q * lax.rsqrt(row_sumsq(q) + NORM_EPS) * (HEAD_DIM ** -0.5)
        k = k * lax.rsqrt(row_sumsq(k) + NORM_EPS)
        a_lane = GDN_A_LANE + j
        beta = beta_all[:, j:j + 1]
        g_col = g_all[:, a_lane:a_lane + 1]
        egc = e_gc[:, a_lane:a_lane + 1]
        kb = k * beta
        k3 = k.astype(BF16).reshape(nc, c, HEAD_DIM)
        kb3 = kb.astype(BF16).reshape(nc, c, HEAD_DIM)
        q3 = q.astype(BF16).reshape(nc, c, HEAD_DIM)
        kk = jnp.einsum("nik,njk->nij", kb3, k3, preferred_element_type=F32)
        qk = jnp.einsum("nik,njk->nij", q3, k3, preferred_element_type=F32)
        bm = g_col.reshape(nc, c, 1) * strict_f
        diff = _bdot(tril3, jnp.concatenate(_split3_bf16(bm), axis=1))
        decay = jnp.exp(diff)
        lower = jnp.where(strict, kk * decay, 0.0)
        intra = jnp.where(causal, qk * decay, 0.0)
        m_pow = -lower
        n_acc = m_pow
        for _ in range(5):
            mb = m_pow.astype(BF16)
            m_pow = _bdot(mb, mb)
            n_acc = n_acc + m_pow + _bdot(n_acc.astype(BF16), m_pow.astype(BF16))
        rhs = jnp.concatenate([v * beta, kb * egc], axis=-1).reshape(nc, c, 2 * HEAD_DIM)
        sol = rhs + _bdot(n_acc.astype(BF16), rhs.astype(BF16))
        u3 = sol[:, :, :HEAD_DIM].astype(BF16)
        w3 = sol[:, :, HEAD_DIM:].astype(BF16)
        ke3 = (k * e_rem[:, a_lane:a_lane + 1]).reshape(nc, c, HEAD_DIM)
        ket3 = jnp.swapaxes(ke3, 1, 2).astype(BF16)
        intra3 = intra.astype(BF16)
        kw_ = _bdot(ket3, w3)
        qw_ = (q * egc).reshape(nc, c, HEAD_DIM) - _bdot(intra3, w3)
        mc_ref[j] = jnp.concatenate([kw_, qw_], axis=1).reshape(nc * (HEAD_DIM + c), HEAD_DIM).astype(BF16)
        ku_ref[j] = _bdot(ket3, u3).reshape(nc * HEAD_DIM, HEAD_DIM)
        iu_ref[j] = _bdot(intra3, u3).reshape(seq, HEAD_DIM).astype(BF16)
        egl_col = e_gl[:, a_lane:a_lane + 1].reshape(nc, c, 1)[:, 0, :]
        egl_ref[j] = jnp.broadcast_to(egl_col, (nc, LANES))

    st_ref[...] = jnp.zeros_like(st_ref)
    onw = onw_ref[...]

    def body(n, carry):
        r0 = pl.multiple_of(n * c, c)
        k0 = pl.multiple_of(n * HEAD_DIM, HEAD_DIM)
        m0 = pl.multiple_of(n * (HEAD_DIM + c), c)
        heads = range(hb)
        state = [st_ref[j] for j in heads]
        ms = [jnp.dot(mc_ref[j, pl.ds(m0, HEAD_DIM + c), :], state[j].astype(BF16),
                      preferred_element_type=F32) for j in heads]
        for j in heads:
            st_ref[j] = (state[j] * egl_ref[j, pl.ds(n, 1), :] - ms[j][:HEAD_DIM]
                         + ku_ref[j, pl.ds(k0, HEAD_DIM), :])
        o = [ms[j][HEAD_DIM:] + iu_ref[j, pl.ds(r0, c), :].astype(F32) for j in heads]
        ss = [row_sumsq(o[j]) for j in heads]
        for j in heads:
            zc = j * B_HEAD_COLS + 3 * HEAD_DIM
            z = proj_ref[0, pl.ds(r0, c), zc:zc + HEAD_DIM].astype(F32)
            on = o[j] * lax.rsqrt(ss[j] * (1.0 / HEAD_DIM) + NORM_EPS) * onw
            y_ref[0, pl.ds(r0, c), j * HEAD_DIM:(j + 1) * HEAD_DIM] = (on * _silu(z)).astype(BF16)
        return carry

    lax.fori_loop(0, nc, body, 0, unroll=2)


def _gdn(proj, ba, convw, alog, dtb, onw, hb=GDN_HEADS_PER_STEP):
    bsz, s, _ = proj.shape
    ng = B_HEADS // hb
    nc = s // B_CHUNK
    return pl.pallas_call(
        functools.partial(_gdn_kernel, hb=hb, seq=s),
        grid=(bsz, ng),
        in_specs=[pl.BlockSpec((1, s, hb * B_HEAD_COLS), lambda b, g: (b, 0, g)),
                  pl.BlockSpec((1, s, LANES), lambda b, g: (b, 0, g)),
                  pl.BlockSpec((B_CONV, hb * B_HEAD_COLS), lambda b, g: (0, g)),
                  pl.BlockSpec((1, 1, LANES), lambda b, g: (g, 0, 0)),
                  pl.BlockSpec((1, 1, LANES), lambda b, g: (g, 0, 0)),
                  pl.BlockSpec((1, HEAD_DIM), lambda b, g: (0, 0))],
        out_specs=pl.BlockSpec((1, s, hb * HEAD_DIM), lambda b, g: (b, 0, g)),
        out_shape=jax.ShapeDtypeStruct((bsz, s, B_HEADS * HEAD_DIM), BF16),
        scratch_shapes=[pltpu.VMEM((hb, nc * (HEAD_DIM + B_CHUNK), HEAD_DIM), BF16),
                        pltpu.VMEM((hb, nc * HEAD_DIM, HEAD_DIM), F32),
                        pltpu.VMEM((hb, s, HEAD_DIM), BF16),
                        pltpu.VMEM((hb, nc, LANES), F32),
                        pltpu.VMEM((hb, HEAD_DIM, HEAD_DIM), F32)],
        compiler_params=_cparams(("arbitrary", "arbitrary")),
        name="gdn",
    )(proj, ba, convw, alog, dtb, onw.reshape(1, HEAD_DIM))


def _mix_kernel(o0_ref, o1_ref, o2_ref, l0_ref, l1_ref, l2_ref, yb_ref, ga_ref, gb_ref, x_ref,
                mod_ref, wa_ref, wb_ref, wo_ref, n2_ref, wr_ref, br_ref,
                x1_ref, h2_ref, route_ref):
    tm = x_ref.shape[1]
    o_refs = (o0_ref, o1_ref, o2_ref)
    ls = [l0_ref[0], l1_ref[0], l2_ref[0]]
    mx = jnp.maximum(jnp.maximum(ls[0], ls[1]), ls[2])
    es = [jnp.exp(l - mx) for l in ls]
    den = es[0] + es[1] + es[2]
    ws = [e / den for e in es]
    heads = []
    for hd in range(A_HEADS_PER_GROUP):
        c0 = hd * HEAD_DIM
        acc = ws[0][:, hd:hd + 1] * o_refs[0][0, :, c0:c0 + HEAD_DIM].astype(F32)
        for g in (1, 2):
            acc = acc + ws[g][:, hd:hd + 1] * o_refs[g][0, :, c0:c0 + HEAD_DIM].astype(F32)
        heads.append(acc)
    ya = jnp.concatenate(heads, axis=1).astype(BF16)
    a = jnp.dot(ya, wa_ref[...], preferred_element_type=F32)
    bm = jnp.dot(yb_ref[0], wb_ref[...], preferred_element_type=F32)
    merged = _sigmoid(ga_ref[0].astype(F32)) * a + _sigmoid(gb_ref[0].astype(F32)) * bm
    mix = jnp.dot(merged.astype(BF16), wo_ref[...], preferred_element_type=F32)
    m = mod_ref[0]
    x1 = x_ref[0] + m[2:3] * mix
    x1_ref[0] = x1
    h2 = _modulated_norm(x1, n2_ref[...], m[4:5], m[3:4])
    h2_ref[0] = h2.astype(BF16)

    lt = lax.dot_general(wr_ref[...], h2, _NT, preferred_element_type=F32,
                         precision=lax.Precision.HIGHEST) + br_ref[...]
    gl = [lt[i:i + 1] for i in range(MOE_GROUPS)]
    gm = jnp.maximum(jnp.maximum(gl[0], gl[1]), jnp.maximum(gl[2], gl[3]))
    gsum = sum(jnp.exp(x - gm) for x in gl)
    p_top = 1.0 / gsum
    sel, taken = [], jnp.zeros(gm.shape, jnp.bool_)
    for i in range(MOE_GROUPS):
        s_i = jnp.logical_and(gl[i] == gm, jnp.logical_not(taken))
        taken = jnp.logical_or(taken, s_i)
        sel.append(s_i)
    ig = []
    for e in range(MOE_EXPERTS_PER_GROUP):
        acc = jnp.zeros(gm.shape, F32)
        for g in range(MOE_GROUPS):
            r = ROUTER_EXPERT_ROW0 + g * MOE_EXPERTS_PER_GROUP + e
            acc = jnp.where(sel[g], lt[r:r + 1], acc)
        ig.append(acc)
    t1 = jnp.maximum(jnp.maximum(ig[0], ig[1]), jnp.maximum(ig[2], ig[3]))
    is1, taken = [], jnp.zeros(gm.shape, jnp.bool_)
    for e in range(MOE_EXPERTS_PER_GROUP):
        s_e = jnp.logical_and(ig[e] == t1, jnp.logical_not(taken))
        taken = jnp.logical_or(taken, s_e)
        is1.append(s_e)
    rest = [jnp.where(is1[e], -jnp.inf, ig[e]) for e in range(MOE_EXPERTS_PER_GROUP)]
    t2 = jnp.maximum(jnp.maximum(rest[0], rest[1]), jnp.maximum(rest[2], rest[3]))
    is2, taken = [], jnp.zeros(gm.shape, jnp.bool_)
    for e in range(MOE_EXPERTS_PER_GROUP):
        s_e = jnp.logical_and(rest[e] == t2, jnp.logical_not(taken))
        taken = jnp.logical_or(taken, s_e)
        is2.append(s_e)
    e2 = jnp.exp(t2 - t1)
    w1 = p_top / (1.0 + e2)
    w2 = p_top * e2 / (1.0 + e2)
    wt = [jnp.where(is1[e], w1, 0.0) + jnp.where(is2[e], w2, 0.0) for e in range(MOE_EXPERTS_PER_GROUP)]
    gid = jnp.zeros(gm.shape, F32)
    for g in range(1, MOE_GROUPS):
        gid = jnp.where(sel[g], float(g), gid)
    rowi = lax.broadcasted_iota(jnp.int32, (ROUTE_ROWS, tm), 0)
    slab = jnp.where(rowi == 0, jnp.broadcast_to(gid, (ROUTE_ROWS, tm)), 0.0)
    for e in range(MOE_EXPERTS_PER_GROUP):
        slab = jnp.where(rowi == 1 + e, jnp.broadcast_to(wt[e], (ROUTE_ROWS, tm)), slab)
    route_ref[0] = slab


def _mix(o_list, lse_list, yb, proj, x, mod, wa, wb, wo, n2w, wr, br, tm=512):
    bsz, s, d = x.shape
    tok = lambda b, i: (b, i, 0)
    full = lambda b, i: (0, 0)
    gate_blk = GATE_COL0 // d
    in_specs = ([pl.BlockSpec((1, tm, A_GROUP_WIDTH), tok)] * 3
                + [pl.BlockSpec((1, tm, LANES), tok)] * 3
                + [pl.BlockSpec((1, tm, d), tok),
                   pl.BlockSpec((1, tm, d), lambda b, i: (b, i, gate_blk)),
                   pl.BlockSpec((1, tm, d), lambda b, i: (b, i, gate_blk + 1)),
                   pl.BlockSpec((1, tm, d), tok),
                   pl.BlockSpec((1, N_MOD, d), lambda b, i: (b, 0, 0)),
                   pl.BlockSpec(wa.shape, full),
                   pl.BlockSpec(wb.shape, full),
                   pl.BlockSpec(wo.shape, full),
                   pl.BlockSpec((1, d), full),
                   pl.BlockSpec(wr.shape, full),
                   pl.BlockSpec(br.shape, full)])
    return pl.pallas_call(
        _mix_kernel,
        grid=(bsz, s // tm),
        in_specs=in_specs,
        out_specs=[pl.BlockSpec((1, tm, d), tok),
                   pl.BlockSpec((1, tm, d), tok),
                   pl.BlockSpec((1, ROUTE_ROWS, tm), lambda b, i: (b, 0, i))],
        out_shape=[jax.ShapeDtypeStruct((bsz, s, d), F32),
                   jax.ShapeDtypeStruct((bsz, s, d), BF16),
                   jax.ShapeDtypeStruct((bsz, ROUTE_ROWS, s), F32)],
        compiler_params=_cparams(("arbitrary", "arbitrary")),
        name="mix",
    )(*o_list, *lse_list, yb, proj, proj, x, mod, wa, wb, wo, n2w.reshape(1, d), wr, br)


def _moe_kernel(h2_ref, route_ref, wg_ref, wu_ref, wd_ref, x1_ref, mod_ref, out_ref, xa_ref, ys_ref, upper_ref):
    tm, blk = MOE_TILE, MOE_BLOCK
    d = h2_ref.shape[1]
    n_slots = ys_ref.shape[0]
    r = route_ref[0]
    rowi = lax.broadcasted_iota(jnp.int32, (ROUTE_ROWS, tm), 0)
    gid = jnp.broadcast_to(r[0:1], (ROUTE_ROWS, tm))
    onehot = jnp.where((gid == rowi.astype(F32)) & (rowi < MOE_GROUPS), 1.0, 0.0)
    oh_b = onehot.astype(BF16)
    @pl.when(pl.program_id(0) == 0)
    def _():
        kk = lax.broadcasted_iota(jnp.int32, (tm, tm), 0)
        tt = lax.broadcasted_iota(jnp.int32, (tm, tm), 1)
        upper_ref[0] = (kk < tt).astype(BF16)
        upper_ref[1] = jnp.ones((tm, tm), BF16)

    before = jnp.dot(oh_b, upper_ref[0], preferred_element_type=F32)
    count = jnp.dot(oh_b, upper_ref[1], preferred_element_type=F32)
    nblk = jnp.floor((count + (blk - 1)) * (1.0 / blk))
    slot = jnp.zeros((1, tm), F32)
    base = jnp.zeros((1, tm), F32)
    for g in range(MOE_GROUPS):
        slot = slot + onehot[g:g + 1] * (base + before[g:g + 1])
        base = base + nblk[g:g + 1] * blk
    slot_i = slot.astype(jnp.int32)

    slab = jnp.where(rowi == 0, jnp.broadcast_to(slot, (ROUTE_ROWS, tm)), r)
    slab_t = jnp.concatenate([slab, jnp.zeros((LANES - ROUTE_ROWS, tm), F32)], axis=0).T
    slot_col = jnp.broadcast_to(slab_t[:, 0:1], (tm, LANES)).astype(jnp.int32)
    cw_hi = slab_t.astype(BF16)
    cw_lo = (slab_t - cw_hi.astype(F32)).astype(BF16)
    xa_ref[:, 0:d] = h2_ref[...]
    xa_ref[:, d:d + LANES] = cw_hi
    xa_ref[:, d + LANES:d + 2 * LANES] = cw_lo

    done = jnp.int32(0)
    for g in range(MOE_GROUPS):
        n_g = (jnp.sum(onehot[g:g + 1]).astype(jnp.int32) + (blk - 1)) // blk

        def body(j, carry, g=g, first=done):
            s0 = pl.multiple_of((first + j) * blk, MOE_ROW_ALIGN)
            rows = lax.broadcasted_iota(jnp.int32, (blk, tm), 0) + s0
            gather = (rows == jnp.broadcast_to(slot_i, (blk, tm))).astype(BF16)
            xb = jnp.dot(gather, xa_ref[...], preferred_element_type=F32)
            hb = xb[:, 0:d].astype(BF16)
            cwb = xb[:, d:d + LANES] + xb[:, d + LANES:d + 2 * LANES]
            experts = range(g * MOE_EXPERTS_PER_GROUP, (g + 1) * MOE_EXPERTS_PER_GROUP)
            gate = [jnp.dot(hb, wg_ref[e], preferred_element_type=F32) for e in experts]
            up = [jnp.dot(hb, wu_ref[e], preferred_element_type=F32) for e in experts]
            hid = [(_silu(gate[i]) * up[i] * cwb[:, 1 + i:2 + i]).astype(BF16)
                   for i in range(MOE_EXPERTS_PER_GROUP)]
            y = jnp.dot(hid[0], wd_ref[experts[0]], preferred_element_type=F32)
            for i in range(1, MOE_EXPERTS_PER_GROUP):
                y = y + jnp.dot(hid[i], wd_ref[experts[i]], preferred_element_type=F32)
            ys_ref[pl.ds(s0, blk), :] = y.astype(BF16)
            return carry

        lax.fori_loop(0, n_g, body, 0)
        done = done + n_g

    def zero_block(j, carry):
        ys_ref[pl.ds(pl.multiple_of(j * blk, MOE_ROW_ALIGN), blk), :] = jnp.zeros((blk, d), BF16)
        return carry

    max_blocks = tm // blk + MOE_GROUPS
    lax.fori_loop(done, max_blocks, zero_block, 0)
    if n_slots > max_blocks * blk:
        ys_ref[max_blocks * blk:n_slots, :] = jnp.zeros((n_slots - max_blocks * blk, d), BF16)
    cols = lax.broadcasted_iota(jnp.int32, (tm, LANES), 1)
    scatter = jnp.concatenate([(cols + c0 == slot_col).astype(BF16) for c0 in range(0, n_slots, LANES)], axis=1)
    moe = jnp.dot(scatter, ys_ref[...], preferred_element_type=F32)
    out_ref[...] = x1_ref[...] + mod_ref[0][5:6] * moe


def _moe(h2, route, wg, wu, wd, x1, mod, seq):
    t, d = h2.shape
    tm = MOE_TILE
    n_slots = -(-((tm // MOE_BLOCK + MOE_GROUPS) * MOE_BLOCK) // LANES) * LANES
    tiles_per_seq = seq // tm
    resident = lambda shape: pl.BlockSpec(shape, lambda i: (0, 0, 0), pipeline_mode=pl.Buffered(1))
    return pl.pallas_call(
        _moe_kernel,
        grid=(t // tm,),
        in_specs=[pl.BlockSpec((tm, d), lambda i: (i, 0)),
                  pl.BlockSpec((1, ROUTE_ROWS, tm), lambda i: (i // tiles_per_seq, 0, i % tiles_per_seq)),
                  resident(wg.shape), resident(wu.shape), resident(wd.shape),
                  pl.BlockSpec((tm, d), lambda i: (i, 0)),
                  pl.BlockSpec((1, N_MOD, d), lambda i: (i // tiles_per_seq, 0, 0))],
        out_specs=pl.BlockSpec((tm, d), lambda i: (i, 0)),
        out_shape=jax.ShapeDtypeStruct((t, d), F32),
        scratch_shapes=[pltpu.VMEM((tm, d + 2 * LANES), BF16),
                        pltpu.VMEM((n_slots, d), BF16),
                        pltpu.VMEM((2, tm, tm), BF16)],
        compiler_params=_cparams(("arbitrary",)),
        name="moe",
    )(h2, route, wg, wu, wd, x1, mod)


def _prep_in_weights(w_in, conv_w, a_log, dt_bias, hb):
    d = w_in.shape[0]
    aw = 3 * A_GROUP_WIDTH
    bw = B_HEADS * HEAD_DIM
    offs = np.cumsum([0, aw, aw, aw, bw, bw, bw, bw, B_HEADS, B_HEADS, d, d])
    qa, ka, va, qb, kb, vb, zb, beta, a_raw, ga, gb = [w_in[:, offs[i]:offs[i + 1]] for i in range(11)]
    cols, conv_cols = [], []
    for h in range(B_HEADS):
        sl = slice(h * HEAD_DIM, (h + 1) * HEAD_DIM)
        cols += [qb[:, sl], kb[:, sl], vb[:, sl], zb[:, sl]]
        conv_cols += [conv_w[:, sl], conv_w[:, bw + h * HEAD_DIM:bw + (h + 1) * HEAD_DIM],
                      conv_w[:, 2 * bw + h * HEAD_DIM:2 * bw + (h + 1) * HEAD_DIM],
                      jnp.zeros((B_CONV, HEAD_DIM), conv_w.dtype)]
    cols += [ga, gb]
    for g in range(len(A_GROUPS)):
        sl = slice(g * A_GROUP_WIDTH, (g + 1) * A_GROUP_WIDTH)
        cols += [qa[:, sl], ka[:, sl], va[:, sl]]
    w1 = jnp.concatenate(cols, axis=1).astype(BF16)
    convw = jnp.concatenate(conv_cols, axis=1)
    ng = B_HEADS // hb
    pad = jnp.zeros((d, LANES - GDN_A_LANE - hb), w_in.dtype)
    gap = jnp.zeros((d, GDN_A_LANE - hb), w_in.dtype)
    ba_cols = []
    vec_pad = lambda v: jnp.concatenate([jnp.zeros((GDN_A_LANE,), F32), v, jnp.zeros((LANES - GDN_A_LANE - hb,), F32)])
    alog, dtb = [], []
    for gi in range(ng):
        sl = slice(gi * hb, (gi + 1) * hb)
        ba_cols += [beta[:, sl], gap, a_raw[:, sl], pad]
        alog.append(vec_pad(a_log[sl].astype(F32)))
        dtb.append(vec_pad(dt_bias[sl].astype(F32)))
    wba = jnp.concatenate(ba_cols, axis=1).astype(BF16)
    return w1, wba, convw, jnp.stack(alog)[:, None, :], jnp.stack(dtb)[:, None, :]


def _prep_router(w_rg, b_rg, w_re, b_re):
    d = w_rg.shape[0]
    n_exp = MOE_GROUPS * MOE_EXPERTS_PER_GROUP
    wr = jnp.concatenate([w_rg.T, jnp.zeros((ROUTER_EXPERT_ROW0 - MOE_GROUPS, d), F32), w_re.T,
                          jnp.zeros((ROUTER_ROWS - ROUTER_EXPERT_ROW0 - n_exp, d), F32)], axis=0)
    br = jnp.concatenate([b_rg, jnp.zeros((ROUTER_EXPERT_ROW0 - MOE_GROUPS,), F32), b_re.reshape(-1),
                          jnp.zeros((ROUTER_ROWS - ROUTER_EXPERT_ROW0 - n_exp,), F32)])
    return wr.astype(F32), br.astype(F32)[:, None]


def kernel(x, c, w_ada, b_ada, norm1_w, w_in, conv_w, a_q_norm_w, a_k_norm_w, b_A_log, b_dt_bias, b_out_norm_w, w_branch_a, w_branch_b, w_o, norm2_w, w_router_group, b_router_group, w_router_expert, b_router_expert, w_exp_gate, w_exp_up, w_exp_down):
    bsz, s, d = x.shape
    hb = GDN_HEADS_PER_STEP
    for i in range(w_ada.shape[0]):
        mod = _ada(c, w_ada[i], b_ada[i]).reshape(bsz, N_MOD, d)
        w1, wba, convw, alog, dtb = _prep_in_weights(w_in[i], conv_w[i], b_A_log[i], b_dt_bias[i], hb)
        proj, ba = _inproj(x, mod, norm1_w[i], w1, wba)
        o_list, lse_list = [], []
        for g in range(len(A_GROUPS)):
            o_g, lse_g = _attention_group(proj, g, a_q_norm_w[i], a_k_norm_w[i])
            o_list.append(o_g)
            lse_list.append(lse_g)
        yb = _gdn(proj, ba, convw, alog, dtb, b_out_norm_w[i], hb)
        wr, br = _prep_router(w_router_group[i], b_router_group[i], w_router_expert[i], b_router_expert[i])
        x1, h2, route = _mix(o_list, lse_list, yb, proj, x, mod, w_branch_a[i].astype(BF16),
                          w_branch_b[i].astype(BF16), w_o[i].astype(BF16), norm2_w[i], wr, br)
        out = _moe(h2.reshape(bsz * s, d), route, w_exp_gate[i].astype(BF16), w_exp_up[i].astype(BF16),
                   w_exp_down[i].astype(BF16), x1.reshape(bsz * s, d), mod, s)
        x = out.reshape(bsz, s, d)
    return x
```
